```python
import math
import jax, jax.numpy as jnp
from jax import lax
import numpy as np

D_MODEL = 1024
BATCH = 32
SEQ = 2048
DEPTH = 1

MIX_WIDTH = D_MODEL
ATTN_WIDTH = D_MODEL // 2
GDN_WIDTH = MIX_WIDTH - ATTN_WIDTH
DIFF_HEAD_DIM = 128
DIFF_QK_DIM = DIFF_HEAD_DIM // 2
DIFF_HEADS = ATTN_WIDTH // DIFF_HEAD_DIM
ATTN_BLOCK = 128
GDN_HEAD_DIM = 128
GDN_HEADS = GDN_WIDTH // GDN_HEAD_DIM
CONV_WIDTH = 4
GDN_CHUNK = 64
IN_COLS = 3 * ATTN_WIDTH + 4 * GDN_WIDTH + 2 * GDN_HEADS
N_KEYS = 128
N_EXPERTS = N_KEYS * N_KEYS
PEER_HEADS = 8
PEER_TOPK = 16
PEER_KEY_DIM = 256
PEER_HALF = PEER_KEY_DIM // 2
PEER_BLOCK = 128
MOD_CHUNKS = 6
NORM_EPS = 1e-6
NEG_INF = -1e30

kernel_name = "hymba_diffattn_gdn_peer_layer"


def rms_norm(x, gain):
    xf = x.astype(jnp.float32)
    y = xf * lax.rsqrt(jnp.mean(xf * xf, axis=-1, keepdims=True) + NORM_EPS)
    return (y * gain.astype(jnp.float32)).astype(x.dtype)


def l2_norm(x):
    xf = x.astype(jnp.float32)
    return (xf * lax.rsqrt(jnp.sum(xf * xf, axis=-1, keepdims=True) + NORM_EPS)).astype(x.dtype)


def causal_depthwise_conv(x, w):
    C = x.shape[-1]
    return lax.conv_general_dilated(
        x, w[:, None, :].astype(x.dtype), window_strides=(1,),
        padding=[(CONV_WIDTH - 1, 0)], dimension_numbers=("NWC", "WIO", "NWC"),
        feature_group_count=C)


def diff_attention(q, k, v, lam):
    S = q.shape[1]
    H = DIFF_HEADS
    scale = DIFF_QK_DIM ** -0.5
    slopes = 2.0 ** (-8.0 * jnp.arange(1, H + 1, dtype=jnp.float32) / H)
    outs = []
    for blk in range(S // ATTN_BLOCK):
        q0 = blk * ATTN_BLOCK
        kv_len = q0 + ATTN_BLOCK
        qb = q[:, q0:kv_len]
        kb = k[:, :kv_len]
        vb = v[:, :kv_len]
        s = jnp.einsum("bqhmd,bkhmd->bhmqk", qb, kb).astype(jnp.float32) * scale
        rel = (q0 + jnp.arange(ATTN_BLOCK))[:, None] - jnp.arange(kv_len)[None, :]
        bias = jnp.where(rel >= 0, -slopes[:, None, None] * rel, NEG_INF)
        p = jax.nn.softmax(s + bias[None, :, None], axis=-1)
        pd = p[:, :, 0] - lam * p[:, :, 1]
        outs.append(jnp.einsum("bhqk,bkhd->bqhd", pd.astype(v.dtype), vb))
    return jnp.concatenate(outs, axis=1)


def gated_delta_rule_chunked(q, k, v, g, beta):
    out_dtype = v.dtype
    q, k, v, g, beta = (t.astype(jnp.float32) for t in (q, k, v, g, beta))
    B, H, S, dk = k.shape
    dv = v.shape[-1]
    C = GDN_CHUNK
    N = S // C
    q = q * (dk ** -0.5)

    def chunks(t):
        return t.reshape(B, H, N, C, *t.shape[3:])

    q, k, v, g, beta = (chunks(t) for t in (q, k, v, g, beta))
    g = jnp.cumsum(g, axis=-1)
    k_beta = k * beta[..., None]
    v_beta = v * beta[..., None]
    incl = jnp.tril(jnp.ones((C, C), dtype=bool))
    strict = jnp.tril(jnp.ones((C, C), dtype=bool), k=-1)
    gdiff = g[..., :, None] - g[..., None, :]
    decay = jnp.where(incl, jnp.exp(jnp.where(incl, gdiff, 0.0)), 0.0)
    L = jnp.where(strict, jnp.einsum("bhnid,bhnjd->bhnij", k_beta, k) * decay, 0.0)
    rhs = jnp.concatenate([v_beta, k_beta * jnp.exp(g)[..., None]], axis=-1)
    sol = lax.linalg.triangular_solve(L, rhs, left_side=True, lower=True, unit_diagonal=True)
    u, w = sol[..., :dv], sol[..., dv:]
    qk = jnp.einsum("bhnid,bhnjd->bhnij", q, k) * decay
    g_last = g[..., -1]

    def step(state, xs):
        q_c, k_c, u_c, w_c, g_c, gl_c, qk_c = xs
        v_new = u_c - jnp.einsum("bhik,bhkv->bhiv", w_c, state)
        o = (jnp.einsum("bhik,bhkv->bhiv", q_c * jnp.exp(g_c)[..., None], state)
             + jnp.einsum("bhij,bhjv->bhiv", qk_c, v_new))
        k_dec = k_c * jnp.exp(gl_c[..., None] - g_c)[..., None]
        state = state * jnp.exp(gl_c)[..., None, None] + jnp.einsum("bhik,bhiv->bhkv", k_dec, v_new)
        return state, o

    xs = tuple(jnp.moveaxis(t, 2, 0) for t in (q, k, u, w, g, g_last, qk))
    state0 = jnp.zeros((B, H, dk, dv), jnp.float32)
    _, o = lax.scan(step, state0, xs)
    o = jnp.moveaxis(o, 0, 2).reshape(B, H, S, dv)
    return o.astype(out_dtype)


def hybrid_mixer(h, w_in, conv_w, a_log, dt_bias, lambda_q1, lambda_k1, lambda_q2, lambda_k2,
                 attn_norm_gain, gdn_norm_gain, w_out, lambda_init):
    B, S, _ = h.shape
    A, G = ATTN_WIDTH, GDN_WIDTH
    proj = h @ w_in
    qa = proj[..., 0:A].reshape(B, S, DIFF_HEADS, 2, DIFF_QK_DIM)
    ka = proj[..., A:2 * A].reshape(B, S, DIFF_HEADS, 2, DIFF_QK_DIM)
    va = proj[..., 2 * A:3 * A].reshape(B, S, DIFF_HEADS, DIFF_HEAD_DIM)
    lam = (jnp.exp(jnp.sum(lambda_q1.astype(jnp.float32) * lambda_k1.astype(jnp.float32)))
           - jnp.exp(jnp.sum(lambda_q2.astype(jnp.float32) * lambda_k2.astype(jnp.float32)))
           + lambda_init)
    oa = diff_attention(qa, ka, va, lam)
    oa = rms_norm(oa, attn_norm_gain) * (1.0 - lambda_init)
    qkv = jax.nn.silu(causal_depthwise_conv(proj[..., 3 * A:3 * A + 3 * G], conv_w))
    qg = l2_norm(qkv[..., 0:G].reshape(B, S, GDN_HEADS, GDN_HEAD_DIM))
    kg = l2_norm(qkv[..., G:2 * G].reshape(B, S, GDN_HEADS, GDN_HEAD_DIM))
    vg = qkv[..., 2 * G:3 * G].reshape(B, S, GDN_HEADS, GDN_HEAD_DIM)
    z = proj[..., 3 * A + 3 * G:3 * A + 4 * G].reshape(B, S, GDN_HEADS, GDN_HEAD_DIM)
    b_lin = proj[..., 3 * A + 4 * G:3 * A + 4 * G + GDN_HEADS].astype(jnp.float32)
    a_lin = proj[..., 3 * A + 4 * G + GDN_HEADS:].astype(jnp.float32)
    beta = jax.nn.sigmoid(b_lin)
    g = -jnp.exp(a_log.astype(jnp.float32)) * jax.nn.softplus(a_lin + dt_bias.astype(jnp.float32))
    og = gated_delta_rule_chunked(
        qg.transpose(0, 2, 1, 3), kg.transpose(0, 2, 1, 3), vg.transpose(0, 2, 1, 3),
        g.transpose(0, 2, 1), beta.transpose(0, 2, 1))
    og = og.transpose(0, 2, 1, 3)
    og = rms_norm(og, gdn_norm_gain) * jax.nn.silu(z)
    y = jnp.concatenate([oa.reshape(B, S, A), og.reshape(B, S, G)], axis=-1)
    return y @ w_out


def peer_ffn(h, w_query, sub_keys, expert_down, expert_up):
    B, S, D = h.shape
    P = PEER_BLOCK
    K = PEER_TOPK
    blocks = h.reshape((B * S) // P, P, D)

    def block_fn(xb):
        q = (xb @ w_query).reshape(P, PEER_HEADS, 2, PEER_HALF)
        s = jnp.einsum("phcd,hcnd->phcn", q, sub_keys)
        s_top, i_top = lax.top_k(s, K)
        cand = s_top[:, :, 0, :, None] + s_top[:, :, 1, None, :]
        cand_idx = i_top[:, :, 0, :, None] * N_KEYS + i_top[:, :, 1, None, :]
        best, pos = lax.top_k(cand.reshape(P, PEER_HEADS, K * K), K)
        idx = jnp.take_along_axis(cand_idx.reshape(P, PEER_HEADS, K * K), pos, axis=-1)
        gates = jax.nn.softmax(best.astype(jnp.float32), axis=-1).astype(xb.dtype)
        idx = idx.reshape(P, PEER_HEADS * K)
        u = expert_down[idx]
        act = jax.nn.gelu(jnp.einsum("pd,ped->pe", xb, u), approximate=False)
        coef = gates.reshape(P, PEER_HEADS * K) * act
        return jnp.einsum("pe,ped->pd", coef, expert_up[idx])

    return lax.map(block_fn, blocks).reshape(B, S, D)


def setup_inputs(seed: int = 0) -> dict:
    key = jax.random.key(seed)
    ks = jax.random.split(key, 24)
    D = D_MODEL

    def nrm(k, shape, scale):
        return jax.random.normal(k, shape, jnp.float32) * scale

    dt = jnp.exp(jax.random.uniform(ks[10], (DEPTH, GDN_HEADS), jnp.float32,
                                    minval=math.log(1e-3), maxval=math.log(1e-1)))
    return {
        "x": nrm(ks[0], (BATCH, SEQ, D), 1.0),
        "c": nrm(ks[1], (BATCH, D), 1.0),
        "w_mod": nrm(ks[2], (DEPTH, D, MOD_CHUNKS * D), 0.5 * D ** -0.5),
        "b_mod": nrm(ks[3], (DEPTH, MOD_CHUNKS * D), 0.01),
        "pre_mix_gain": 1.0 + nrm(ks[4], (DEPTH, D), 0.02),
        "post_mix_gain": 1.0 + nrm(ks[5], (DEPTH, D), 0.02),
        "pre_ffn_gain": 1.0 + nrm(ks[6], (DEPTH, D), 0.02),
        "post_ffn_gain": 1.0 + nrm(ks[7], (DEPTH, D), 0.02),
        "w_in": nrm(ks[8], (DEPTH, D, IN_COLS), D ** -0.5),
        "conv_w": nrm(ks[9], (DEPTH, CONV_WIDTH, 3 * GDN_WIDTH), CONV_WIDTH ** -0.5),
        "a_log": jnp.log(jax.random.uniform(ks[11], (DEPTH, GDN_HEADS), jnp.float32,
                                             minval=1.0, maxval=16.0)),
        "dt_bias": dt + jnp.log(-jnp.expm1(-dt)),
        "lambda_q1": nrm(ks[12], (DEPTH, DIFF_QK_DIM), 0.1),
        "lambda_k1": nrm(ks[13], (DEPTH, DIFF_QK_DIM), 0.1),
        "lambda_q2": nrm(ks[14], (DEPTH, DIFF_QK_DIM), 0.1),
        "lambda_k2": nrm(ks[15], (DEPTH, DIFF_QK_DIM), 0.1),
        "attn_norm_gain": 1.0 + nrm(ks[16], (DEPTH, DIFF_HEAD_DIM), 0.02),
        "gdn_norm_gain": 1.0 + nrm(ks[17], (DEPTH, GDN_HEAD_DIM), 0.02),
        "w_out": nrm(ks[18], (DEPTH, MIX_WIDTH, D), MIX_WIDTH ** -0.5),
        "w_query": nrm(ks[19], (DEPTH, D, PEER_HEADS * PEER_KEY_DIM), D ** -0.5),
        "sub_keys": nrm(ks[20], (DEPTH, PEER_HEADS, 2, N_KEYS, PEER_HALF), PEER_HALF ** -0.5),
        "expert_down": nrm(ks[21], (DEPTH, N_EXPERTS, D), D ** -0.5),
        "expert_up": nrm(ks[22], (DEPTH, N_EXPERTS, D), D ** -0.5),
    }


def reference(x, c, w_mod, b_mod, pre_mix_gain, post_mix_gain, pre_ffn_gain, post_ffn_gain,
              w_in, conv_w, a_log, dt_bias, lambda_q1, lambda_k1, lambda_q2, lambda_k2,
              attn_norm_gain, gdn_norm_gain, w_out, w_query, sub_keys, expert_down, expert_up):
    for layer in range(DEPTH):
        mod = jax.nn.silu(c) @ w_mod[layer] + b_mod[layer]
        shift_m, scale_m, gate_m, shift_f, scale_f, gate_f = (
            m[:, None, :] for m in jnp.split(mod, MOD_CHUNKS, axis=-1))
        lambda_init = 0.8 - 0.6 * math.exp(-0.3 * layer)
        h = rms_norm(x, pre_mix_gain[layer]) * (1.0 + scale_m) + shift_m
        y = hybrid_mixer(h, w_in[layer], conv_w[layer], a_log[layer], dt_bias[layer],
                         lambda_q1[layer], lambda_k1[layer], lambda_q2[layer], lambda_k2[layer],
                         attn_norm_gain[layer], gdn_norm_gain[layer], w_out[layer], lambda_init)
        x = x + gate_m * rms_norm(y, post_mix_gain[layer])
        h = rms_norm(x, pre_ffn_gain[layer]) * (1.0 + scale_f) + shift_f
        y = peer_ffn(h, w_query[layer], sub_keys[layer], expert_down[layer], expert_up[layer])
        x = x + gate_f * rms_norm(y, post_ffn_gain[layer])
    return x
```

```python
import functools
import math

import jax
import jax.numpy as jnp
import numpy as np
from jax import lax
from jax.experimental import pallas as pl
from jax.experimental.pallas import tpu as pltpu

F32 = jnp.float32
BF16 = jnp.bfloat16
I32 = jnp.int32

NORM_EPS = 1e-6
NEG_INF = -1e30

DIFF_HEAD_DIM = 128
DIFF_QK_DIM = 64
GDN_HEAD_DIM = 128
CONV_WIDTH = 4
GDN_CHUNK = 64
N_KEYS = 128
PEER_HEADS = 8
PEER_TOPK = 16
PEER_KEY_DIM = 256
PEER_HALF = 128
MOD_CHUNKS = 6

LANES = 128
SUBLANES = 8
VMEM_LIMIT_BYTES = 56 * 1024 * 1024

NT_DIMS = (((1,), (1,)), ((), ()))


def _dot(a, b):
    return jnp.dot(a, b, preferred_element_type=F32)


def _dot_nt(a, b):
    return lax.dot_general(a, b, NT_DIMS, preferred_element_type=F32)


def _split_bf16(a):
    hi = a.astype(BF16)
    lo = (a - hi.astype(F32)).astype(BF16)
    return hi, lo


def _dot_hi(a, b):
    ah, al = _split_bf16(a)
    bh, bl = _split_bf16(b)
    return _dot(ah, bh) + _dot(ah, bl) + _dot(al, bh)


def _rms(x):
    return x * lax.rsqrt(jnp.mean(x * x, axis=-1, keepdims=True) + NORM_EPS)


def _params(*sem):
    return pltpu.CompilerParams(dimension_semantics=sem, vmem_limit_bytes=VMEM_LIMIT_BYTES)


def _mod_kernel(c_ref, w_ref, b_ref, o_ref):
    c = c_ref[...]
    sc = c * jax.nn.sigmoid(c)
    o_ref[...] = _dot(sc.astype(BF16), w_ref[...]) + b_ref[...]


def _mod_call(c, w_mod_bf16, b_mod):
    B, D = c.shape
    n = w_mod_bf16.shape[1]
    return pl.pallas_call(
        _mod_kernel,
        out_shape=jax.ShapeDtypeStruct((B, n), F32),
        grid=(n // D,),
        in_specs=[
            pl.BlockSpec((B, D), lambda j: (0, 0)),
            pl.BlockSpec((D, D), lambda j: (0, j)),
            pl.BlockSpec((1, D), lambda j: (0, j)),
        ],
        out_specs=pl.BlockSpec((B, D), lambda j: (0, j)),
        compiler_params=_params("arbitrary"),
        name="mod",
    )(c, w_mod_bf16, b_mod.reshape(1, n))


def _inproj_kernel(x_ref, mod_ref, gain_ref, w_ref, wba_ref, oa_ref, og_ref, oba_ref, *, n_attn, n_gdn, n_step):
    m = mod_ref[0]
    h = _rms(x_ref[...]) * gain_ref[...]
    h = h * (1.0 + m[1:2]) + m[0:1]
    hb = h.astype(BF16)
    for n0 in range(0, n_attn, n_step):
        oa_ref[:, n0:n0 + n_step] = _dot(hb, w_ref[:, n0:n0 + n_step]).astype(BF16)
    for n0 in range(0, n_gdn, n_step):
        og_ref[:, n0:n0 + n_step] = _dot(hb, w_ref[:, n_attn + n0:n_attn + n0 + n_step]).astype(BF16)
    oba_ref[...] = _dot_nt(wba_ref[...], hb)


def _inproj_call(x2, mod3, gain, w_main, w_ba_t, *, seq, n_attn, n_gdn, tm):
    T, D = x2.shape
    tiles_per_b = seq // tm
    nba = w_ba_t.shape[0]
    kern = functools.partial(_inproj_kernel, n_attn=n_attn, n_gdn=n_gdn, n_step=512)
    return pl.pallas_call(
        kern,
        out_shape=(
            jax.ShapeDtypeStruct((T, n_attn), BF16),
            jax.ShapeDtypeStruct((T, n_gdn), BF16),
            jax.ShapeDtypeStruct((nba, T), F32),
        ),
        grid=(T // tm,),
        in_specs=[
            pl.BlockSpec((tm, D), lambda i: (i, 0)),
            pl.BlockSpec((1, MOD_CHUNKS, D), lambda i: (i // tiles_per_b, 0, 0)),
            pl.BlockSpec((1, D), lambda i: (0, 0)),
            pl.BlockSpec((D, n_attn + n_gdn), lambda i: (0, 0)),
            pl.BlockSpec((nba, D), lambda i: (0, 0)),
        ],
        out_specs=(
            pl.BlockSpec((tm, n_attn), lambda i: (i, 0)),
            pl.BlockSpec((tm, n_gdn), lambda i: (i, 0)),
            pl.BlockSpec((nba, tm), lambda i: (0, i)),
        ),
        compiler_params=_params("arbitrary"),
        name="inproj",
    )(x2, mod3, gain.reshape(1, D), w_main, w_ba_t)


def _attn_kernel(slopes_ref, q_ref, k_ref, v_ref, lam_ref, gain_ref, o_ref, *, lambda_init, blk):
    h = pl.program_id(1)
    qi = pl.program_id(2)
    slope = slopes_ref[h]
    scale = DIFF_QK_DIM ** -0.5
    lp = lam_ref[...]
    lam = (jnp.exp(jnp.sum(lp[0:1] * lp[1:2], axis=-1, keepdims=True))
           - jnp.exp(jnp.sum(lp[2:3] * lp[3:4], axis=-1, keepdims=True)) + lambda_init)

    q = q_ref[...]
    lane = lax.broadcasted_iota(I32, q.shape, 1)
    zero = jnp.zeros_like(q)
    q0 = jnp.where(lane < DIFF_QK_DIM, q, zero)
    q1 = jnp.where(lane >= DIFF_QK_DIM, q, zero)
    qpos = qi * blk + lax.broadcasted_iota(I32, (blk, 1), 0)

    def block(j, carry, masked):
        m0, l0, a0, m1, l1, a1 = carry
        start = pl.multiple_of(j * blk, blk)
        k = k_ref[pl.ds(start, blk), :]
        v = v_ref[pl.ds(start, blk), :]
        kpos = start + lax.broadcasted_iota(I32, (1, blk), 1)
        bias = slope * kpos.astype(F32)
        s0 = _dot_nt(q0, k) * scale + bias
        s1 = _dot_nt(q1, k) * scale + bias
        if masked:
            ok = kpos <= qpos
            s0 = jnp.where(ok, s0, NEG_INF)
            s1 = jnp.where(ok, s1, NEG_INF)
        n0 = jnp.maximum(m0, jnp.max(s0, axis=-1, keepdims=True))
        n1 = jnp.maximum(m1, jnp.max(s1, axis=-1, keepdims=True))
        p0 = jnp.exp(s0 - n0)
        p1 = jnp.exp(s1 - n1)
        c0 = jnp.exp(m0 - n0)
        c1 = jnp.exp(m1 - n1)
        l0 = c0 * l0 + jnp.sum(p0, axis=-1, keepdims=True)
        l1 = c1 * l1 + jnp.sum(p1, axis=-1, keepdims=True)
        pv = _dot(jnp.concatenate([p0, p1], axis=0).astype(BF16), v)
        a0 = c0 * a0 + pv[:blk]
        a1 = c1 * a1 + pv[blk:]
        return n0, l0, a0, n1, l1, a1

    col = jnp.full((blk, 1), NEG_INF, F32)
    zcol = jnp.zeros((blk, 1), F32)
    zacc = jnp.zeros((blk, DIFF_HEAD_DIM), F32)
    carry = (col, zcol, zacc, col, zcol, zacc)
    carry = lax.fori_loop(0, qi, lambda j, c: block(j, c, False), carry)
    _, l0, a0, _, l1, a1 = block(qi, carry, True)
    o = a0 / l0 - lam * (a1 / l1)
    o = _rms(o) * gain_ref[...] * (1.0 - lambda_init)
    o_ref[...] = o.astype(BF16)


def _attn_call(slopes, qkv, lam_params, gain, *, batch, seq, heads, lambda_init, blk):
    T = qkv.shape[0]
    nq = seq // blk
    kern = functools.partial(_attn_kernel, lambda_init=lambda_init, blk=blk)
    return pl.pallas_call(
        kern,
        out_shape=jax.ShapeDtypeStruct((T, heads * DIFF_HEAD_DIM), BF16),
        grid=(batch, heads, nq),
        in_specs=[
            pl.BlockSpec(memory_space=pltpu.SMEM),
            pl.BlockSpec((blk, DIFF_HEAD_DIM), lambda b, h, i: (b * nq + i, h)),
            pl.BlockSpec((seq, DIFF_HEAD_DIM), lambda b, h, i: (b, heads + h)),
            pl.BlockSpec((seq, DIFF_HEAD_DIM), lambda b, h, i: (b, 2 * heads + h)),
            pl.BlockSpec((4, DIFF_QK_DIM), lambda b, h, i: (0, 0)),
            pl.BlockSpec((1, DIFF_HEAD_DIM), lambda b, h, i: (0, 0)),
        ],
        out_specs=pl.BlockSpec((blk, DIFF_HEAD_DIM), lambda b, h, i: (b * nq + i, h)),
        compiler_params=_params("arbitrary", "arbitrary", "arbitrary"),
        name="diff_attn",
    )(slopes, qkv, qkv, qkv, lam_params, gain.reshape(1, DIFF_HEAD_DIM))


GDN_BLOCK = 2 * GDN_CHUNK
INV_BLOCK = 16


def _unit_lower_inverse(L, eye, in_blk):
    Ld = jnp.where(in_blk, L, 0.0)
    Lo = L - Ld
    P = eye - Ld
    X = _dot_hi(Ld, Ld)
    P = P + _dot_hi(P, X)
    X = _dot_hi(X, X)
    P = P + _dot_hi(P, X)
    X = _dot_hi(X, X)
    Td = P + _dot_hi(P, X)
    M = _dot_hi(Td, Lo)
    M2 = _dot_hi(M, M)
    R = eye - M
    R = R + _dot_hi(R, M2)
    return _dot_hi(R, Td)


def _gdn_kernel(alog_ref, dtb_ref, q_ref, k_ref, v_ref, z_ref, cq_ref, ck_ref, cv_ref, ba_ref, gain_ref,
                o_ref, qs, ks, vs, *, heads):
    h = pl.program_id(1)
    seq = q_ref.shape[0]
    dk = GDN_HEAD_DIM
    nb = seq // GDN_BLOCK
    C = GDN_CHUNK

    row = lax.broadcasted_iota(I32, (seq, dk), 0)

    def conv_silu(x_ref, w_ref):
        x = x_ref[...].astype(F32)
        w = w_ref[...]
        y = w[CONV_WIDTH - 1:CONV_WIDTH] * x
        for j in range(CONV_WIDTH - 1):
            s = CONV_WIDTH - 1 - j
            xs = jnp.where(row >= s, pltpu.roll(x, s, 0), 0.0)
            y = y + w[j:j + 1] * xs
        return y * jax.nn.sigmoid(y)

    def l2n(x):
        return x * lax.rsqrt(jnp.sum(x * x, axis=-1, keepdims=True) + NORM_EPS)

    qs[...] = l2n(conv_silu(q_ref, cq_ref)) * (dk ** -0.5)
    ks[...] = l2n(conv_silu(k_ref, ck_ref))
    vs[...] = conv_silu(v_ref, cv_ref)

    ii = lax.broadcasted_iota(I32, (GDN_BLOCK, GDN_BLOCK), 0)
    jj = lax.broadcasted_iota(I32, (GDN_BLOCK, GDN_BLOCK), 1)
    same = (ii // C) == (jj // C)
    incl = same & (jj <= ii)
    strict = same & (jj < ii)
    upper = same & (ii <= jj)
    eye_b = ii == jj
    eye = eye_b.astype(F32)
    in_blk = (ii // INV_BLOCK) == (jj // INV_BLOCK)
    last = same & ((jj % C) == (C - 1))

    a_coef = jnp.exp(jnp.full((1, GDN_BLOCK), alog_ref[h], F32))
    dt_b = jnp.full((1, GDN_BLOCK), dtb_ref[h], F32)
    gain = gain_ref[...]
    zeros_c = jnp.zeros((C, dk), F32)
    ba_row_id = lax.broadcasted_iota(I32, (2 * heads, GDN_BLOCK), 0)

    def body(n, state):
        st = pl.multiple_of(n * GDN_BLOCK, GDN_BLOCK)
        ba = ba_ref[:, pl.ds(st, GDN_BLOCK)]
        b_lin = jnp.sum(jnp.where(ba_row_id == h, ba, 0.0), axis=0, keepdims=True)
        a_lin = jnp.sum(jnp.where(ba_row_id == heads + h, ba, 0.0), axis=0, keepdims=True)
        beta_row = jax.nn.sigmoid(b_lin)
        g_row = -a_coef * jax.nn.softplus(a_lin + dt_b)
        g_b = jnp.broadcast_to(g_row, (GDN_BLOCK, GDN_BLOCK))
        beta_b = jnp.broadcast_to(beta_row, (GDN_BLOCK, GDN_BLOCK))
        cs_col = jnp.sum(jnp.where(incl, g_b, 0.0), axis=-1, keepdims=True)
        g_col = jnp.sum(jnp.where(eye_b, g_b, 0.0), axis=-1, keepdims=True)
        beta_col = jnp.sum(jnp.where(eye_b, beta_b, 0.0), axis=-1, keepdims=True)
        cs_row = jnp.sum(jnp.where(upper, g_col, 0.0), axis=0, keepdims=True)
        gl_col = jnp.sum(jnp.where(last, cs_row, 0.0), axis=-1, keepdims=True)
        decay = jnp.where(incl, jnp.exp(jnp.where(incl, cs_col - cs_row, 0.0)), 0.0)

        q = qs[pl.ds(st, GDN_BLOCK), :]
        k = ks[pl.ds(st, GDN_BLOCK), :]
        v = vs[pl.ds(st, GDN_BLOCK), :]
        kb = k * beta_col
        kbf = k.astype(BF16)
        L = jnp.where(strict, _dot_nt(kb.astype(BF16), kbf) * decay, 0.0)
        T = _unit_lower_inverse(L, eye, in_blk)
        rhs = jnp.concatenate([v * beta_col, kb * jnp.exp(cs_col)], axis=1)
        sol = _dot_hi(T, rhs)
        u = sol[:, :dk]
        w = sol[:, dk:]
        qk = (_dot_nt(q.astype(BF16), kbf) * decay).astype(BF16)
        qg = (q * jnp.exp(cs_col)).astype(BF16)
        kdec = k * jnp.exp(gl_col - cs_col)
        egl = jnp.exp(gl_col)
        wb = w.astype(BF16)

        sb = state.astype(BF16)
        vn0 = u[:C] - _dot(wb[:C], sb)
        o0 = _dot(qg[:C], sb) + _dot(qk[:C], jnp.concatenate([vn0, zeros_c], axis=0).astype(BF16))
        state = state * egl[0:1] + _dot(kdec[:C].T.astype(BF16), vn0.astype(BF16))
        sb = state.astype(BF16)
        vn1 = u[C:] - _dot(wb[C:], sb)
        o1 = _dot(qg[C:], sb) + _dot(qk[C:], jnp.concatenate([vn0, vn1], axis=0).astype(BF16))
        state = state * egl[C:C + 1] + _dot(kdec[C:].T.astype(BF16), vn1.astype(BF16))

        o = jnp.concatenate([o0, o1], axis=0)
        z = z_ref[pl.ds(st, GDN_BLOCK), :].astype(F32)
        o = _rms(o) * gain * (z * jax.nn.sigmoid(z))
        o_ref[pl.ds(st, GDN_BLOCK), :] = o.astype(BF16)
        return state

    lax.fori_loop(0, nb, body, jnp.zeros((dk, dk), F32))


def _gdn_call(a_log, dt_bias, gq, conv_w, ba_row, gain, *, batch, seq, heads):
    T = gq.shape[0]
    dk = GDN_HEAD_DIM
    kern = functools.partial(_gdn_kernel, heads=heads)
    seq_spec = lambda off: pl.BlockSpec((seq, dk), lambda b, h: (b, off + h))
    conv_spec = lambda off: pl.BlockSpec((CONV_WIDTH, dk), lambda b, h: (0, off + h))
    return pl.pallas_call(
        kern,
        out_shape=jax.ShapeDtypeStruct((T, heads * dk), BF16),
        grid=(batch, heads),
        in_specs=[
            pl.BlockSpec(memory_space=pltpu.SMEM),
            pl.BlockSpec(memory_space=pltpu.SMEM),
            seq_spec(0), seq_spec(heads), seq_spec(2 * heads), seq_spec(3 * heads),
            conv_spec(0), conv_spec(heads), conv_spec(2 * heads),
            pl.BlockSpec((2 * heads, seq), lambda b, h: (0, b)),
            pl.BlockSpec((1, dk), lambda b, h: (0, 0)),
        ],
        out_specs=pl.BlockSpec((seq, dk), lambda b, h: (b, h)),
        scratch_shapes=[pltpu.VMEM((seq, dk), F32)] * 3,
        compiler_params=_params("arbitrary", "arbitrary"),
        name="gdn",
    )(a_log, dt_bias, gq, gq, gq, gq, conv_w, conv_w, conv_w, ba_row, gain.reshape(1, dk))


def _outproj_kernel(oa_ref, og_ref, x_ref, mod_ref, pg_ref, fg_ref, wo_ref, wq_ref, sk_ref,
                    x1_ref, h2_ref, sc_ref):
    m = mod_ref[0]
    y = _dot(jnp.concatenate([oa_ref[...], og_ref[...]], axis=1), wo_ref[...])
    x1 = x_ref[...] + m[2:3] * (_rms(y) * pg_ref[...])
    x1_ref[...] = x1
    h2 = (_rms(x1) * fg_ref[...]) * (1.0 + m[4:5]) + m[3:4]
    h2b = h2.astype(BF16)
    h2_ref[...] = h2b
    nhc = sk_ref.shape[0]
    for hc in range(nhc):
        qp = _dot(h2b, wq_ref[:, hc * PEER_HALF:(hc + 1) * PEER_HALF]).astype(BF16)
        sc_ref[hc] = _dot_nt(sk_ref[hc], qp)


def _outproj_call(oa, og, x2, mod3, post_gain, ffn_gain, w_out, w_query, sub_keys, *, seq, tm):
    T, D = x2.shape
    tiles_per_b = seq // tm
    nhc = sub_keys.shape[0]
    a = oa.shape[1]
    g = og.shape[1]
    return pl.pallas_call(
        _outproj_kernel,
        out_shape=(
            jax.ShapeDtypeStruct((T, D), F32),
            jax.ShapeDtypeStruct((T, D), BF16),
            jax.ShapeDtypeStruct((nhc, N_KEYS, T), F32),
        ),
        grid=(T // tm,),
        in_specs=[
            pl.BlockSpec((tm, a), lambda i: (i, 0)),
            pl.BlockSpec((tm, g), lambda i: (i, 0)),
            pl.BlockSpec((tm, D), lambda i: (i, 0)),
            pl.BlockSpec((1, MOD_CHUNKS, D), lambda i: (i // tiles_per_b, 0, 0)),
            pl.BlockSpec((1, D), lambda i: (0, 0)),
            pl.BlockSpec((1, D), lambda i: (0, 0)),
            pl.BlockSpec((a + g, D), lambda i: (0, 0)),
            pl.BlockSpec((D, nhc * PEER_HALF), lambda i: (0, 0)),
            pl.BlockSpec((nhc, N_KEYS, PEER_HALF), lambda i: (0, 0, 0)),
        ],
        out_specs=(
            pl.BlockSpec((tm, D), lambda i: (i, 0)),
            pl.BlockSpec((tm, D), lambda i: (i, 0)),
            pl.BlockSpec((nhc, N_KEYS, tm), lambda i: (0, 0, i)),
        ),
        compiler_params=_params("arbitrary"),
        name="outproj",
    )(oa, og, x2, mod3, post_gain.reshape(1, D), ffn_gain.reshape(1, D), w_out, w_query, sub_keys)


N_CAND_ROWS = 80


def _topk_kernel(sc_ref, g_ref, i_ref, j_ref, gt, it, jt, *, cols):
    K = PEER_TOPK
    neg = jnp.float32(-jnp.inf)
    n_iota = lax.broadcasted_iota(I32, (N_KEYS, LANES), 0).astype(F32)
    r16 = lax.broadcasted_iota(I32, (K, LANES), 0)

    r = lax.broadcasted_iota(I32, (N_CAND_ROWS, LANES), 0)
    grp = r // SUBLANES
    sub = r % SUBLANES
    cand_a = jnp.where(grp < 8, grp, jnp.where(grp == 8, 0, 8 + sub))
    cand_b = jnp.where(grp < 8, sub, jnp.where(grp == 8, 8 + sub, 0))
    cand_ok = (cand_a + 1) * (cand_b + 1) <= K
    cand_pos = (cand_a * K + cand_b).astype(F32)

    def top16(s):
        def step(k, carry):
            cur, vals, idxs = carry
            m = jnp.max(cur, axis=0, keepdims=True)
            idx = jnp.min(jnp.where(cur == m, n_iota, float(N_KEYS)), axis=0, keepdims=True)
            vals = jnp.where(r16 == k, m, vals)
            idxs = jnp.where(r16 == k, idx, idxs)
            cur = jnp.where(n_iota == idx, neg, cur)
            return cur, vals, idxs
        _, vals, idxs = lax.fori_loop(0, K, step, (s, jnp.zeros((K, LANES), F32), jnp.zeros((K, LANES), F32)),
                                      unroll=True)
        return vals, idxs

    def per_head(hh, c0):
        v0, i0 = top16(sc_ref[2 * hh, :, pl.ds(c0, LANES)])
        v1, i1 = top16(sc_ref[2 * hh + 1, :, pl.ds(c0, LANES)])
        groups = [v0[a:a + 1] + v1[0:SUBLANES] for a in range(8)]
        groups.append(v0[0:1] + v1[SUBLANES:K])
        groups.append(v0[SUBLANES:K] + v1[0:1])
        cand = jnp.where(cand_ok, jnp.concatenate(groups, axis=0), neg)

        def step(k, carry):
            cur, vals, poss = carry
            m = jnp.max(cur, axis=0, keepdims=True)
            p = jnp.min(jnp.where(cur == m, cand_pos, float(K * K)), axis=0, keepdims=True)
            vals = jnp.where(r16 == k, m, vals)
            poss = jnp.where(r16 == k, p, poss)
            cur = jnp.where(cand_pos == p, neg, cur)
            return cur, vals, poss
        _, best, pos = lax.fori_loop(0, K, step, (cand, jnp.zeros((K, LANES), F32), jnp.zeros((K, LANES), F32)),
                                     unroll=True)
        pos = pos.astype(I32)
        pa = pos // K
        pb = pos % K
        ki = jnp.zeros((K, LANES), F32)
        kj = jnp.zeros((K, LANES), F32)
        for a in range(K):
            ki = jnp.where(pa == a, i0[a:a + 1], ki)
            kj = jnp.where(pb == a, i1[a:a + 1], kj)
        e = jnp.exp(best - best[0:1])
        gates = e / jnp.sum(e, axis=0, keepdims=True)
        return gates, ki, kj

    for c in range(cols):
        c0 = c * LANES
        for hh in range(PEER_HEADS):
            gates, ki, kj = per_head(hh, c0)
            gt[hh * K:(hh + 1) * K, c0:c0 + LANES] = gates
            it[hh * K:(hh + 1) * K, c0:c0 + LANES] = ki
            jt[hh * K:(hh + 1) * K, c0:c0 + LANES] = kj
    g_ref[...] = gt[...].T
    i_ref[...] = it[...].T.astype(I32)
    j_ref[...] = jt[...].T.astype(I32)


def _topk_call(scores_t, *, tt):
    nhc, nk, T = scores_t.shape
    hk = PEER_HEADS * PEER_TOPK
    kern = functools.partial(_topk_kernel, cols=tt // LANES)
    return pl.pallas_call(
        kern,
        out_shape=(
            jax.ShapeDtypeStruct((T, hk), F32),
            jax.ShapeDtypeStruct((T, hk), I32),
            jax.ShapeDtypeStruct((T, hk), I32),
        ),
        grid=(T // tt,),
        in_specs=[pl.BlockSpec((nhc, nk, tt), lambda i: (0, 0, i))],
        out_specs=(
            pl.BlockSpec((tt, hk), lambda i: (i, 0)),
            pl.BlockSpec((tt, hk), lambda i: (i, 0)),
            pl.BlockSpec((tt, hk), lambda i: (i, 0)),
        ),
        scratch_shapes=[pltpu.VMEM((hk, tt), F32)] * 3,
        compiler_params=_params("arbitrary"),
        name="peer_topk",
    )(scores_t)


def _wbuild_kernel(g_ref, i_ref, j_ref, w_ref, ws, *, tmw, pitch):
    sub = lax.broadcasted_iota(I32, (N_KEYS, LANES), 0)

    def body(t, carry):
        g = g_ref[pl.ds(t, 1), :]
        ki = i_ref[pl.ds(t, 1), :]
        kj = j_ref[pl.ds(t, 1), :]
        pi = jnp.where(sub == ki, g, 0.0).astype(BF16)
        pj = jnp.where(sub == kj, 1.0, 0.0).astype(BF16)
        ws[pl.ds(t, N_KEYS, stride=pitch), :] = _dot_nt(pi, pj)
        return carry

    lax.fori_loop(0, tmw, body, 0)

    def copy(i, carry):
        start = pl.multiple_of(i * pitch, SUBLANES)
        w_ref[i] = ws[pl.ds(start, tmw), :].astype(BF16)
        return carry

    lax.fori_loop(0, N_KEYS, copy, 0)


def _wbuild_call(gates, ki, kj, *, tmw):
    T, hk = gates.shape
    pitch = tmw + SUBLANES
    kern = functools.partial(_wbuild_kernel, tmw=tmw, pitch=pitch)
    return pl.pallas_call(
        kern,
        out_shape=jax.ShapeDtypeStruct((N_KEYS, T, N_KEYS), BF16),
        grid=(T // tmw,),
        in_specs=[pl.BlockSpec((tmw, hk), lambda i: (i, 0))] * 3,
        out_specs=pl.BlockSpec((N_KEYS, tmw, N_KEYS), lambda i: (0, i, 0)),
        scratch_shapes=[pltpu.VMEM((N_KEYS * pitch, N_KEYS), F32)],
        compiler_params=_params("arbitrary"),
        name="peer_gate_table",
    )(gates, ki, kj)


def _expert_kernel(h_ref, d_ref, u_ref, w_ref, x1_ref, mod_ref, gain_ref, o_ref, acc, *, ic):
    e = pl.program_id(1)

    @pl.when(e == 0)
    def _():
        acc[...] = jnp.zeros_like(acc)

    s = _dot_nt(h_ref[...], d_ref[...])
    act = 0.5 * s * (1.0 + lax.erf(s * (2.0 ** -0.5)))
    w = jnp.concatenate([w_ref[i] for i in range(ic)], axis=1).astype(F32)
    acc[...] += _dot((act * w).astype(BF16), u_ref[...])

    @pl.when(e == pl.num_programs(1) - 1)
    def _():
        m = mod_ref[0]
        o_ref[...] = x1_ref[...] + m[5:6] * (_rms(acc[...]) * gain_ref[...])


def _expert_call(h2, down, up, wtab, x1, mod3, gain, *, seq, tm, ic):
    T, D = h2.shape
    tiles_per_b = seq // tm
    ec = ic * N_KEYS
    kern = functools.partial(_expert_kernel, ic=ic)
    return pl.pallas_call(
        kern,
        out_shape=jax.ShapeDtypeStruct((T, D), F32),
        grid=(T // tm, N_KEYS // ic),
        in_specs=[
            pl.BlockSpec((tm, D), lambda i, e: (i, 0)),
            pl.BlockSpec((ec, D), lambda i, e: (e, 0)),
            pl.BlockSpec((ec, D), lambda i, e: (e, 0)),
            pl.BlockSpec((ic, tm, N_KEYS), lambda i, e: (e, i, 0)),
            pl.BlockSpec((tm, D), lambda i, e: (i, 0)),
            pl.BlockSpec((1, MOD_CHUNKS, D), lambda i, e: (i // tiles_per_b, 0, 0)),
            pl.BlockSpec((1, D), lambda i, e: (0, 0)),
        ],
        out_specs=pl.BlockSpec((tm, D), lambda i, e: (i, 0)),
        scratch_shapes=[pltpu.VMEM((tm, D), F32)],
        compiler_params=_params("arbitrary", "arbitrary"),
        name="peer_experts",
    )(h2, down, up, wtab, x1, mod3, gain.reshape(1, D))


def _tile(n, pref):
    t = min(n, pref)
    assert n % t == 0, (n, pref)
    return t


def kernel(x, c, w_mod, b_mod, pre_mix_gain, post_mix_gain, pre_ffn_gain, post_ffn_gain, w_in, conv_w, a_log, dt_bias, lambda_q1, lambda_k1, lambda_q2, lambda_k2, attn_norm_gain, gdn_norm_gain, w_out, w_query, sub_keys, expert_down, expert_up):
    B, S, D = x.shape
    T = B * S
    depth = w_mod.shape[0]
    n_attn_w = D // 2
    n_gdn_w = D - n_attn_w
    attn_heads = n_attn_w // DIFF_HEAD_DIM
    gdn_heads = n_gdn_w // GDN_HEAD_DIM
    n_attn = 3 * n_attn_w
    n_gdn = 4 * n_gdn_w
    assert w_in.shape[2] == n_attn + n_gdn + 2 * gdn_heads
    assert S % GDN_BLOCK == 0

    slopes = jnp.asarray(2.0 ** (-8.0 * np.arange(1, attn_heads + 1) / attn_heads), F32)
    x2 = x.reshape(T, D)
    for layer in range(depth):
        lambda_init = 0.8 - 0.6 * math.exp(-0.3 * layer)
        mod = _mod_call(c, w_mod[layer].astype(BF16), b_mod[layer])
        mod3 = mod.reshape(B, MOD_CHUNKS, D)

        w_main = w_in[layer, :, :n_attn + n_gdn].astype(BF16)
        w_ba_t = w_in[layer, :, n_attn + n_gdn:].T.astype(BF16)
        qkv_a, qkvz_g, ba_row = _inproj_call(x2, mod3, pre_mix_gain[layer], w_main, w_ba_t,
                                             seq=S, n_attn=n_attn, n_gdn=n_gdn, tm=_tile(S, 512))

        lam_params = jnp.stack([lambda_q1[layer], lambda_k1[layer], lambda_q2[layer], lambda_k2[layer]])
        oa = _attn_call(slopes, qkv_a, lam_params, attn_norm_gain[layer], batch=B, seq=S, heads=attn_heads,
                        lambda_init=lambda_init, blk=_tile(S, 256))
        og = _gdn_call(a_log[layer], dt_bias[layer], qkvz_g, conv_w[layer], ba_row, gdn_norm_gain[layer],
                       batch=B, seq=S, heads=gdn_heads)

        sk = sub_keys[layer].reshape(PEER_HEADS * 2, N_KEYS, PEER_HALF).astype(BF16)
        x1, h2, scores_t = _outproj_call(oa, og, x2, mod3, post_mix_gain[layer], pre_ffn_gain[layer],
                                         w_out[layer].astype(BF16), w_query[layer].astype(BF16), sk,
                                         seq=S, tm=_tile(S, 512))
        gates, ki, kj = _topk_call(scores_t, tt=_tile(T, 512))
        wtab = _wbuild_call(gates, ki, kj, tmw=_tile(T, 256))
        x2 = _expert_call(h2, expert_down[layer].astype(BF16), expert_up[layer].astype(BF16), wtab, x1, mod3,
                          post_ffn_gain[layer], seq=S, tm=_tile(S, 1024), ic=4)
    return x2.reshape(B, S, D)
```

```python
import functools
import math

import jax
import jax.numpy as jnp
import numpy as np
from jax import lax
from jax.experimental import pallas as pl
from jax.experimental.pallas import tpu as pltpu

F32 = jnp.float32
BF16 = jnp.bfloat16
I32 = jnp.int32

NORM_EPS = 1e-6
NEG_INF = -1e30

DIFF_HEAD_DIM = 128
DIFF_QK_DIM = 64
GDN_HEAD_DIM = 128
CONV_WIDTH = 4
GDN_CHUNK = 64
N_KEYS = 128
PEER_HEADS = 8
PEER_TOPK = 16
PEER_KEY_DIM = 256
PEER_HALF = 128
MOD_CHUNKS = 6

LANES = 128
SUBLANES = 8
VMEM_LIMIT_BYTES = 56 * 1024 * 1024

NT_DIMS = (((1,), (1,)), ((), ()))


def _dot(a, b):
    return jnp.dot(a, b, preferred_element_type=F32)


def _dot_nt(a, b):
    return lax.dot_general(a, b, NT_DIMS, preferred_element_type=F32)


def _split_bf16(a):
    hi = a.astype(BF16)
    lo = (a - hi.astype(F32)).astype(BF16)
    return hi, lo


def _dot_hi(a, b):
    ah, al = _split_bf16(a)
    bh, bl = _split_bf16(b)
    return _dot(ah, bh) + _dot(ah, bl) + _dot(al, bh)


def _rms(x):
    return x * lax.rsqrt(jnp.mean(x * x, axis=-1, keepdims=True) + NORM_EPS)


def _params(*sem):
    return pltpu.CompilerParams(dimension_semantics=sem, vmem_limit_bytes=VMEM_LIMIT_BYTES)


def _mod_kernel(c_ref, w_ref, b_ref, o_ref):
    c = c_ref[...]
    sc = c * jax.nn.sigmoid(c)
    o_ref[...] = _dot(sc.astype(BF16), w_ref[...]) + b_ref[...]


def _mod_call(c, w_mod_bf16, b_mod):
    B, D = c.shape
    n = w_mod_bf16.shape[1]
    return pl.pallas_call(
        _mod_kernel,
        out_shape=jax.ShapeDtypeStruct((B, n), F32),
        grid=(n // D,),
        in_specs=[
            pl.BlockSpec((B, D), lambda j: (0, 0)),
            pl.BlockSpec((D, D), lambda j: (0, j)),
            pl.BlockSpec((1, D), lambda j: (0, j)),
        ],
        out_specs=pl.BlockSpec((B, D), lambda j: (0, j)),
        compiler_params=_params("arbitrary"),
        name="mod",
    )(c, w_mod_bf16, b_mod.reshape(1, n))


def _inproj_kernel(x_ref, mod_ref, gain_ref, w_ref, wba_ref, oa_ref, og_ref, oba_ref, *, n_attn, n_gdn, n_step):
    m = mod_ref[0]
    h = _rms(x_ref[...]) * gain_ref[...]
    h = h * (1.0 + m[1:2]) + m[0:1]
    hb = h.astype(BF16)
    for n0 in range(0, n_attn, n_step):
        oa_ref[:, n0:n0 + n_step] = _dot(hb, w_ref[:, n0:n0 + n_step]).astype(BF16)
    for n0 in range(0, n_gdn, n_step):
        og_ref[:, n0:n0 + n_step] = _dot(hb, w_ref[:, n_attn + n0:n_attn + n0 + n_step]).astype(BF16)
    oba_ref[...] = _dot_nt(wba_ref[...], hb)


def _inproj_call(x2, mod3, gain, w_main, w_ba_t, *, seq, n_attn, n_gdn, tm):
    T, D = x2.shape
    tiles_per_b = seq // tm
    nba = w_ba_t.shape[0]
    kern = functools.partial(_inproj_kernel, n_attn=n_attn, n_gdn=n_gdn, n_step=512)
    return pl.pallas_call(
        kern,
        out_shape=(
            jax.ShapeDtypeStruct((T, n_attn), BF16),
            jax.ShapeDtypeStruct((T, n_gdn), BF16),
            jax.ShapeDtypeStruct((nba, T), F32),
        ),
        grid=(T // tm,),
        in_specs=[
            pl.BlockSpec((tm, D), lambda i: (i, 0)),
            pl.BlockSpec((1, MOD_CHUNKS, D), lambda i: (i // tiles_per_b, 0, 0)),
            pl.BlockSpec((1, D), lambda i: (0, 0)),
            pl.BlockSpec((D, n_attn + n_gdn), lambda i: (0, 0)),
            pl.BlockSpec((nba, D), lambda i: (0, 0)),
        ],
        out_specs=(
            pl.BlockSpec((tm, n_attn), lambda i: (i, 0)),
            pl.BlockSpec((tm, n_gdn), lambda i: (i, 0)),
            pl.BlockSpec((nba, tm), lambda i: (0, i)),
        ),
        compiler_params=_params("arbitrary"),
        name="inproj",
    )(x2, mod3, gain.reshape(1, D), w_main, w_ba_t)


def _attn_kernel(slopes_ref, q_ref, k_ref, v_ref, lam_ref, gain_ref, o_ref, *, lambda_init, blk):
    h = pl.program_id(1)
    qi = pl.program_id(2)
    slope = slopes_ref[h]
    scale = DIFF_QK_DIM ** -0.5
    lp = lam_ref[...]
    lam = (jnp.exp(jnp.sum(lp[0:1] * lp[1:2], axis=-1, keepdims=True))
           - jnp.exp(jnp.sum(lp[2:3] * lp[3:4], axis=-1, keepdims=True)) + lambda_init)

    q = q_ref[...]
    lane = lax.broadcasted_iota(I32, q.shape, 1)
    zero = jnp.zeros_like(q)
    q0 = jnp.where(lane < DIFF_QK_DIM, q, zero)
    q1 = jnp.where(lane >= DIFF_QK_DIM, q, zero)
    qpos = qi * blk + lax.broadcasted_iota(I32, (blk, 1), 0)

    def block(j, carry, masked):
        m0, l0, a0, m1, l1, a1 = carry
        start = pl.multiple_of(j * blk, blk)
        k = k_ref[pl.ds(start, blk), :]
        v = v_ref[pl.ds(start, blk), :]
        kpos = start + lax.broadcasted_iota(I32, (1, blk), 1)
        bias = slope * kpos.astype(F32)
        s0 = _dot_nt(q0, k) * scale + bias
        s1 = _dot_nt(q1, k) * scale + bias
        if masked:
            ok = kpos <= qpos
            s0 = jnp.where(ok, s0, NEG_INF)
            s1 = jnp.where(ok, s1, NEG_INF)
        n0 = jnp.maximum(m0, jnp.max(s0, axis=-1, keepdims=True))
        n1 = jnp.maximum(m1, jnp.max(s1, axis=-1, keepdims=True))
        p0 = jnp.exp(s0 - n0)
        p1 = jnp.exp(s1 - n1)
        c0 = jnp.exp(m0 - n0)
        c1 = jnp.exp(m1 - n1)
        l0 = c0 * l0 + jnp.sum(p0, axis=-1, keepdims=True)
        l1 = c1 * l1 + jnp.sum(p1, axis=-1, keepdims=True)
        pv = _dot(jnp.concatenate([p0, p1], axis=0).astype(BF16), v)
        a0 = c0 * a0 + pv[:blk]
        a1 = c1 * a1 + pv[blk:]
        return n0, l0, a0, n1, l1, a1

    col = jnp.full((blk, 1), NEG_INF, F32)
    zcol = jnp.zeros((blk, 1), F32)
    zacc = jnp.zeros((blk, DIFF_HEAD_DIM), F32)
    carry = (col, zcol, zacc, col, zcol, zacc)
    carry = lax.fori_loop(0, qi, lambda j, c: block(j, c, False), carry)
    _, l0, a0, _, l1, a1 = block(qi, carry, True)
    o = a0 / l0 - lam * (a1 / l1)
    o = _rms(o) * gain_ref[...] * (1.0 - lambda_init)
    o_ref[...] = o.astype(BF16)


def _attn_call(slopes, qkv, lam_params, gain, *, batch, seq, heads, lambda_init, blk):
    T = qkv.shape[0]
    nq = seq // blk
    kern = functools.partial(_attn_kernel, lambda_init=lambda_init, blk=blk)
    return pl.pallas_call(
        kern,
        out_shape=jax.ShapeDtypeStruct((T, heads * DIFF_HEAD_DIM), BF16),
        grid=(batch, heads, nq),
        in_specs=[
            pl.BlockSpec(memory_space=pltpu.SMEM),
            pl.BlockSpec((blk, DIFF_HEAD_DIM), lambda b, h, i: (b * nq + i, h)),
            pl.BlockSpec((seq, DIFF_HEAD_DIM), lambda b, h, i: (b, heads + h)),
            pl.BlockSpec((seq, DIFF_HEAD_DIM), lambda b, h, i: (b, 2 * heads + h)),
            pl.BlockSpec((4, DIFF_QK_DIM), lambda b, h, i: (0, 0)),
            pl.BlockSpec((1, DIFF_HEAD_DIM), lambda b, h, i: (0, 0)),
        ],
        out_specs=pl.BlockSpec((blk, DIFF_HEAD_DIM), lambda b, h, i: (b * nq + i, h)),
        compiler_params=_params("arbitrary", "arbitrary", "arbitrary"),
        name="diff_attn",
    )(slopes, qkv, qkv, qkv, lam_params, gain.reshape(1, DIFF_HEAD_DIM))


GDN_BLOCK = 2 * GDN_CHUNK
INV_BLOCK = 16


def _each(f, *lists):
    return [f(*args) for args in zip(*lists)]


def _unit_lower_inverse(Ls, eye, in_blk):
    Ld = _each(lambda L: jnp.where(in_blk, L, 0.0), Ls)
    Lo = _each(lambda L, ld: L - ld, Ls, Ld)
    P = _each(lambda ld: eye - ld, Ld)
    X = _each(_dot_hi, Ld, Ld)
    for _ in range(2):
        P = _each(lambda p, x: p + _dot_hi(p, x), P, X)
        X = _each(_dot_hi, X, X)
    Td = _each(lambda p, x: p + _dot_hi(p, x), P, X)
    M = _each(_dot_hi, Td, Lo)
    M2 = _each(_dot_hi, M, M)
    R = _each(lambda m: eye - m, M)
    R = _each(lambda r, m2: r + _dot_hi(r, m2), R, M2)
    return _each(_dot_hi, R, Td)


def _gdn_kernel(alog_ref, dtb_ref, q_ref, k_ref, v_ref, z_ref, cq_ref, ck_ref, cv_ref, ba_ref, gain_ref,
                o_ref, qs, ks, vs, *, heads):
    seq = q_ref.shape[0]
    dk = GDN_HEAD_DIM
    nb = seq // GDN_BLOCK
    C = GDN_CHUNK

    row = lax.broadcasted_iota(I32, (seq, dk), 0)

    def conv_silu(x_ref, w_ref, lo):
        x = x_ref[:, lo:lo + dk].astype(F32)
        w = w_ref[:, lo:lo + dk]
        y = w[CONV_WIDTH - 1:CONV_WIDTH] * x
        for j in range(CONV_WIDTH - 1):
            s = CONV_WIDTH - 1 - j
            xs = jnp.where(row >= s, pltpu.roll(x, s, 0), 0.0)
            y = y + w[j:j + 1] * xs
        return y * jax.nn.sigmoid(y)

    def l2n(x):
        return x * lax.rsqrt(jnp.sum(x * x, axis=-1, keepdims=True) + NORM_EPS)

    for hd in range(heads):
        lo = hd * dk
        qs[:, lo:lo + dk] = l2n(conv_silu(q_ref, cq_ref, lo)) * (dk ** -0.5)
        ks[:, lo:lo + dk] = l2n(conv_silu(k_ref, ck_ref, lo))
        vs[:, lo:lo + dk] = conv_silu(v_ref, cv_ref, lo)

    ii = lax.broadcasted_iota(I32, (GDN_BLOCK, GDN_BLOCK), 0)
    jj = lax.broadcasted_iota(I32, (GDN_BLOCK, GDN_BLOCK), 1)
    same = (ii // C) == (jj // C)
    incl = same & (jj <= ii)
    strict = same & (jj < ii)
    upper = same & (ii <= jj)
    eye_b = ii == jj
    eye = eye_b.astype(F32)
    in_blk = (ii // INV_BLOCK) == (jj // INV_BLOCK)
    last = same & ((jj % C) == (C - 1))

    a_coef = [jnp.exp(jnp.full((1, GDN_BLOCK), alog_ref[hd], F32)) for hd in range(heads)]
    dt_b = [jnp.full((1, GDN_BLOCK), dtb_ref[hd], F32) for hd in range(heads)]
    gain = gain_ref[...]
    zeros_c = jnp.zeros((C, dk), F32)

    def gates(hd, ba):
        beta_row = jax.nn.sigmoid(ba[hd:hd + 1])
        g_row = -a_coef[hd] * jax.nn.softplus(ba[heads + hd:heads + hd + 1] + dt_b[hd])
        g_b = jnp.broadcast_to(g_row, (GDN_BLOCK, GDN_BLOCK))
        beta_b = jnp.broadcast_to(beta_row, (GDN_BLOCK, GDN_BLOCK))
        cs_col = jnp.sum(jnp.where(incl, g_b, 0.0), axis=-1, keepdims=True)
        g_col = jnp.sum(jnp.where(eye_b, g_b, 0.0), axis=-1, keepdims=True)
        beta_col = jnp.sum(jnp.where(eye_b, beta_b, 0.0), axis=-1, keepdims=True)
        cs_row = jnp.sum(jnp.where(upper, g_col, 0.0), axis=0, keepdims=True)
        gl_col = jnp.sum(jnp.where(last, cs_row, 0.0), axis=-1, keepdims=True)
        decay = jnp.where(incl, jnp.exp(jnp.where(incl, cs_col - cs_row, 0.0)), 0.0)
        return beta_col, cs_col, gl_col, decay

    def chunk_step(r0, prev_vn, state, u, wb, qg, qk, kdec, egl):
        sb = _each(lambda s: s.astype(BF16), state)
        vn = _each(lambda u_, w_, s_: u_[r0:r0 + C] - _dot(w_[r0:r0 + C], s_), u, wb, sb)
        vfull = _each(lambda p, v_: jnp.concatenate([v_, zeros_c] if p is None else [p, v_], axis=0).astype(BF16),
                      prev_vn, vn)
        o = _each(lambda g_, s_, k_, vf: _dot(g_[r0:r0 + C], s_) + _dot(k_[r0:r0 + C], vf), qg, sb, qk, vfull)
        state = _each(lambda s, e, kd, v_: s * e[r0:r0 + 1] + _dot(kd[r0:r0 + C].T.astype(BF16), v_.astype(BF16)),
                      state, egl, kdec, vn)
        return vn, o, state

    def body(n, states):
        st = pl.multiple_of(n * GDN_BLOCK, GDN_BLOCK)
        ba = ba_ref[:, pl.ds(st, GDN_BLOCK)]
        hs = list(range(heads))
        beta_col, cs_col, gl_col, decay = zip(*[gates(hd, ba) for hd in hs])
        q = [qs[pl.ds(st, GDN_BLOCK), hd * dk:(hd + 1) * dk] for hd in hs]
        k = [ks[pl.ds(st, GDN_BLOCK), hd * dk:(hd + 1) * dk] for hd in hs]
        v = [vs[pl.ds(st, GDN_BLOCK), hd * dk:(hd + 1) * dk] for hd in hs]
        kb = _each(lambda k_, b: k_ * b, k, beta_col)
        kbf = _each(lambda k_: k_.astype(BF16), k)
        L = _each(lambda kb_, kf, d: jnp.where(strict, _dot_nt(kb_.astype(BF16), kf) * d, 0.0), kb, kbf, decay)
        T = _unit_lower_inverse(L, eye, in_blk)
        rhs = _each(lambda v_, b, kb_, c: jnp.concatenate([v_ * b, kb_ * jnp.exp(c)], axis=1),
                    v, beta_col, kb, cs_col)
        sol = _each(_dot_hi, T, rhs)
        u = _each(lambda s: s[:, :dk], sol)
        wb = _each(lambda s: s[:, dk:].astype(BF16), sol)
        qk = _each(lambda q_, kf, d: (_dot_nt(q_.astype(BF16), kf) * d).astype(BF16), q, kbf, decay)
        qg = _each(lambda q_, c: (q_ * jnp.exp(c)).astype(BF16), q, cs_col)
        kdec = _each(lambda k_, gl, c: k_ * jnp.exp(gl - c), k, gl_col, cs_col)
        egl = _each(jnp.exp, gl_col)

        vn0, o0, states = chunk_step(0, [None] * heads, list(states), u, wb, qg, qk, kdec, egl)
        _, o1, states = chunk_step(C, vn0, states, u, wb, qg, qk, kdec, egl)

        for hd in hs:
            o = jnp.concatenate([o0[hd], o1[hd]], axis=0)
            z = z_ref[pl.ds(st, GDN_BLOCK), hd * dk:(hd + 1) * dk].astype(F32)
            o = _rms(o) * gain * (z * jax.nn.sigmoid(z))
            o_ref[pl.ds(st, GDN_BLOCK), hd * dk:(hd + 1) * dk] = o.astype(BF16)
        return tuple(states)

    lax.fori_loop(0, nb, body, tuple(jnp.zeros((dk, dk), F32) for _ in range(heads)))


def _gdn_call(a_log, dt_bias, gq, conv_w, ba_row, gain, *, batch, seq, heads):
    T = gq.shape[0]
    dk = GDN_HEAD_DIM
    wd = heads * dk
    kern = functools.partial(_gdn_kernel, heads=heads)
    seq_spec = lambda blk: pl.BlockSpec((seq, wd), lambda b: (b, blk))
    conv_spec = lambda blk: pl.BlockSpec((CONV_WIDTH, wd), lambda b: (0, blk))
    return pl.pallas_call(
        kern,
        out_shape=jax.ShapeDtypeStruct((T, wd), BF16),
        grid=(batch,),
        in_specs=[
            pl.BlockSpec(memory_space=pltpu.SMEM),
            pl.BlockSpec(memory_space=pltpu.SMEM),
            seq_spec(0), seq_spec(1), seq_spec(2), seq_spec(3),
            conv_spec(0), conv_spec(1), conv_spec(2),
            pl.BlockSpec((2 * heads, seq), lambda b: (0, b)),
            pl.BlockSpec((1, dk), lambda b: (0, 0)),
        ],
        out_specs=pl.BlockSpec((seq, wd), lambda b: (b, 0)),
        scratch_shapes=[pltpu.VMEM((seq, wd), F32)] * 3,
        compiler_params=_params("arbitrary"),
        name="gdn",
    )(a_log, dt_bias, gq, gq, gq, gq, conv_w, conv_w, conv_w, ba_row, gain.reshape(1, dk))


def _fold_kernel(wq_ref, sk_ref, o_ref):
    ah, al = _split_bf16(wq_ref[...])
    bh, bl = _split_bf16(sk_ref[0])
    o_ref[...] = (_dot_nt(ah, bh) + _dot_nt(ah, bl) + _dot_nt(al, bh)).astype(BF16)


def _fold_call(w_query, sub_keys):
    D = w_query.shape[0]
    nhc = sub_keys.shape[0]
    return pl.pallas_call(
        _fold_kernel,
        out_shape=jax.ShapeDtypeStruct((D, nhc * N_KEYS), BF16),
        grid=(nhc,),
        in_specs=[
            pl.BlockSpec((D, PEER_HALF), lambda j: (0, j)),
            pl.BlockSpec((1, N_KEYS, PEER_HALF), lambda j: (j, 0, 0)),
        ],
        out_specs=pl.BlockSpec((D, N_KEYS), lambda j: (0, j)),
        compiler_params=_params("arbitrary"),
        name="peer_fold",
    )(w_query, sub_keys)


def _outproj_kernel(oa_ref, og_ref, x_ref, mod_ref, pg_ref, fg_ref, wo_ref, wf_ref,
                    x1_ref, h2_ref, sc_ref, *, n_step):
    m = mod_ref[0]
    y = _dot(jnp.concatenate([oa_ref[...], og_ref[...]], axis=1), wo_ref[...])
    x1 = x_ref[...] + m[2:3] * (_rms(y) * pg_ref[...])
    x1_ref[...] = x1
    h2 = (_rms(x1) * fg_ref[...]) * (1.0 + m[4:5]) + m[3:4]
    h2b = h2.astype(BF16)
    h2_ref[...] = h2b
    for n0 in range(0, wf_ref.shape[1], n_step):
        sc_ref[:, n0:n0 + n_step] = _dot(h2b, wf_ref[:, n0:n0 + n_step])


def _outproj_call(oa, og, x2, mod3, post_gain, ffn_gain, w_out, w_fold, *, seq, tm):
    T, D = x2.shape
    tiles_per_b = seq // tm
    ns = w_fold.shape[1]
    a = oa.shape[1]
    g = og.shape[1]
    return pl.pallas_call(
        functools.partial(_outproj_kernel, n_step=512),
        out_shape=(
            jax.ShapeDtypeStruct((T, D), F32),
            jax.ShapeDtypeStruct((T, D), BF16),
            jax.ShapeDtypeStruct((T, ns), F32),
        ),
        grid=(T // tm,),
        in_specs=[
            pl.BlockSpec((tm, a), lambda i: (i, 0)),
            pl.BlockSpec((tm, g), lambda i: (i, 0)),
            pl.BlockSpec((tm, D), lambda i: (i, 0)),
            pl.BlockSpec((1, MOD_CHUNKS, D), lambda i: (i // tiles_per_b, 0, 0)),
            pl.BlockSpec((1, D), lambda i: (0, 0)),
            pl.BlockSpec((1, D), lambda i: (0, 0)),
            pl.BlockSpec((a + g, D), lambda i: (0, 0)),
            pl.BlockSpec((D, ns), lambda i: (0, 0)),
        ],
        out_specs=(
            pl.BlockSpec((tm, D), lambda i: (i, 0)),
            pl.BlockSpec((tm, D), lambda i: (i, 0)),
            pl.BlockSpec((tm, ns), lambda i: (i, 0)),
        ),
        compiler_params=_params("arbitrary"),
        name="outproj",
    )(oa, og, x2, mod3, post_gain.reshape(1, D), ffn_gain.reshape(1, D), w_out, w_fold)


N_CAND_ROWS = 80


def _topk_kernel(sc_ref, g_ref, i_ref, j_ref, gt, it, jt, *, cols):
    K = PEER_TOPK
    neg = jnp.float32(-jnp.inf)
    n_iota = lax.broadcasted_iota(I32, (N_KEYS, LANES), 0).astype(F32)
    r16 = lax.broadcasted_iota(I32, (K, LANES), 0)

    r = lax.broadcasted_iota(I32, (N_CAND_ROWS, LANES), 0)
    grp = r // SUBLANES
    sub = r % SUBLANES
    cand_a = jnp.where(grp < 8, grp, jnp.where(grp == 8, 0, 8 + sub))
    cand_b = jnp.where(grp < 8, sub, jnp.where(grp == 8, 8 + sub, 0))
    cand_ok = (cand_a + 1) * (cand_b + 1) <= K
    cand_pos = (cand_a * K + cand_b).astype(F32)

    def top16(s):
        def step(k, carry):
            cur, vals, idxs = carry
            m = jnp.max(cur, axis=0, keepdims=True)
            idx = jnp.min(jnp.where(cur == m, n_iota, float(N_KEYS)), axis=0, keepdims=True)
            vals = jnp.where(r16 == k, m, vals)
            idxs = jnp.where(r16 == k, idx, idxs)
            cur = jnp.where(n_iota == idx, neg, cur)
            return cur, vals, idxs
        _, vals, idxs = lax.fori_loop(0, K, step, (s, jnp.zeros((K, LANES), F32), jnp.zeros((K, LANES), F32)),
                                      unroll=True)
        return vals, idxs

    def keys_by_tokens(hc, c0):
        return sc_ref[c0:c0 + LANES, hc * N_KEYS:(hc + 1) * N_KEYS].T

    def per_head(hh, c0):
        v0, i0 = top16(keys_by_tokens(2 * hh, c0))
        v1, i1 = top16(keys_by_tokens(2 * hh + 1, c0))
        groups = [v0[a:a + 1] + v1[0:SUBLANES] for a in range(8)]
        groups.append(v0[0:1] + v1[SUBLANES:K])
        groups.append(v0[SUBLANES:K] + v1[0:1])
        cand = jnp.where(cand_ok, jnp.concatenate(groups, axis=0), neg)

        def step(k, carry):
            cur, vals, poss = carry
            m = jnp.max(cur, axis=0, keepdims=True)
            p = jnp.min(jnp.where(cur == m, cand_pos, float(K * K)), axis=0, keepdims=True)
            vals = jnp.where(r16 == k, m, vals)
            poss = jnp.where(r16 == k, p, poss)
            cur = jnp.where(cand_pos == p, neg, cur)
            return cur, vals, poss
        _, best, pos = lax.fori_loop(0, K, step, (cand, jnp.zeros((K, LANES), F32), jnp.zeros((K, LANES), F32)),
                                     unroll=True)
        pos = pos.astype(I32)
        pa = pos // K
        pb = pos % K
        ki = jnp.zeros((K, LANES), F32)
        kj = jnp.zeros((K, LANES), F32)
        for a in range(K):
            ki = jnp.where(pa == a, i0[a:a + 1], ki)
            kj = jnp.where(pb == a, i1[a:a + 1], kj)
        e = jnp.exp(best - best[0:1])
        gates = e / jnp.sum(e, axis=0, keepdims=True)
        return gates, ki, kj

    for c in range(cols):
        c0 = c * LANES
        for hh in range(PEER_HEADS):
            gates, ki, kj = per_head(hh, c0)
            gt[hh * K:(hh + 1) * K, c0:c0 + LANES] = gates
            it[hh * K:(hh + 1) * K, c0:c0 + LANES] = ki
            jt[hh * K:(hh + 1) * K, c0:c0 + LANES] = kj
    g_ref[...] = gt[...].T
    i_ref[...] = it[...].T.astype(I32)
    j_ref[...] = jt[...].T.astype(I32)


def _topk_call(scores, *, tt):
    T, ns = scores.shape
    hk = PEER_HEADS * PEER_TOPK
    kern = functools.partial(_topk_kernel, cols=tt // LANES)
    return pl.pallas_call(
        kern,
        out_shape=(
            jax.ShapeDtypeStruct((T, hk), F32),
            jax.ShapeDtypeStruct((T, hk), I32),
            jax.ShapeDtypeStruct((T, hk), I32),
        ),
        grid=(T // tt,),
        in_specs=[pl.BlockSpec((tt, ns), lambda i: (i, 0))],
        out_specs=(
            pl.BlockSpec((tt, hk), lambda i: (i, 0)),
            pl.BlockSpec((tt, hk), lambda i: (i, 0)),
            pl.BlockSpec((tt, hk), lambda i: (i, 0)),
        ),
        scratch_shapes=[pltpu.VMEM((hk, tt), F32)] * 3,
        compiler_params=_params("arbitrary"),
        name="peer_topk",
    )(scores)


def _wbuild_kernel(g_ref, i_ref, j_ref, w_ref, ws, *, tmw, pitch):
    sub = lax.broadcasted_iota(I32, (N_KEYS, LANES), 0)

    def body(t, carry):
        g = g_ref[pl.ds(t, 1), :]
        ki = i_ref[pl.ds(t, 1), :]
        kj = j_ref[pl.ds(t, 1), :]
        pi = jnp.where(sub == ki, g, 0.0).astype(BF16)
        pj = jnp.where(sub == kj, 1.0, 0.0).astype(BF16)
        ws[pl.ds(t, N_KEYS, stride=pitch), :] = _dot_nt(pi, pj)
        return carry

    lax.fori_loop(0, tmw, body, 0, unroll=8)

    def copy(i, carry):
        start = pl.multiple_of(i * pitch, SUBLANES)
        w_ref[i] = ws[pl.ds(start, tmw), :].astype(BF16)
        return carry

    lax.fori_loop(0, N_KEYS, copy, 0)


def _wbuild_call(gates, ki, kj, *, tmw):
    T, hk = gates.shape
    pitch = tmw + SUBLANES
    kern = functools.partial(_wbuild_kernel, tmw=tmw, pitch=pitch)
    return pl.pallas_call(
        kern,
        out_shape=jax.ShapeDtypeStruct((N_KEYS, T, N_KEYS), BF16),
        grid=(T // tmw,),
        in_specs=[pl.BlockSpec((tmw, hk), lambda i: (i, 0))] * 3,
        out_specs=pl.BlockSpec((N_KEYS, tmw, N_KEYS), lambda i: (0, i, 0)),
        scratch_shapes=[pltpu.VMEM((N_KEYS * pitch, N_KEYS), F32)],
        compiler_params=_params("arbitrary"),
        name="peer_gate_table",
    )(gates, ki, kj)


def _expert_kernel(h_ref, d_ref, u_ref, w_ref, x1_ref, mod_ref, gain_ref, o_ref, acc, *, ic):
    e = pl.program_id(1)

    @pl.when(e == 0)
    def _():
        acc[...] = jnp.zeros_like(acc)

    s = _dot_nt(h_ref[...], d_ref[...])
    act = 0.5 * s * (1.0 + lax.erf(s * (2.0 ** -0.5)))
    w = jnp.concatenate([w_ref[i] for i in range(ic)], axis=1).astype(F32)
    acc[...] += _dot((act * w).astype(BF16), u_ref[...])

    @pl.when(e == pl.num_programs(1) - 1)
    def _():
        m = mod_ref[0]
        o_ref[...] = x1_ref[...] + m[5:6] * (_rms(acc[...]) * gain_ref[...])


def _expert_call(h2, down, up, wtab, x1, mod3, gain, *, seq, tm, ic):
    T, D = h2.shape
    tiles_per_b = seq // tm
    ec = ic * N_KEYS
    kern = functools.partial(_expert_kernel, ic=ic)
    return pl.pallas_call(
        kern,
        out_shape=jax.ShapeDtypeStruct((T, D), F32),
        grid=(T // tm, N_KEYS // ic),
        in_specs=[
            pl.BlockSpec((tm, D), lambda i, e: (i, 0)),
            pl.BlockSpec((ec, D), lambda i, e: (e, 0)),
            pl.BlockSpec((ec, D), lambda i, e: (e, 0)),
            pl.BlockSpec((ic, tm, N_KEYS), lambda i, e: (e, i, 0)),
            pl.BlockSpec((tm, D), lambda i, e: (i, 0)),
            pl.BlockSpec((1, MOD_CHUNKS, D), lambda i, e: (i // tiles_per_b, 0, 0)),
            pl.BlockSpec((1, D), lambda i, e: (0, 0)),
        ],
        out_specs=pl.BlockSpec((tm, D), lambda i, e: (i, 0)),
        scratch_shapes=[pltpu.VMEM((tm, D), F32)],
        compiler_params=_params("arbitrary", "arbitrary"),
        name="peer_experts",
    )(h2, down, up, wtab, x1, mod3, gain.reshape(1, D))


def _tile(n, pref):
    t = min(n, pref)
    assert n % t == 0, (n, pref)
    return t


def kernel(x, c, w_mod, b_mod, pre_mix_gain, post_mix_gain, pre_ffn_gain, post_ffn_gain, w_in, conv_w, a_log, dt_bias, lambda_q1, lambda_k1, lambda_q2, lambda_k2, attn_norm_gain, gdn_norm_gain, w_out, w_query, sub_keys, expert_down, expert_up):
    B, S, D = x.shape
    T = B * S
    depth = w_mod.shape[0]
    n_attn_w = D // 2
    n_gdn_w = D - n_attn_w
    attn_heads = n_attn_w // DIFF_HEAD_DIM
    gdn_heads = n_gdn_w // GDN_HEAD_DIM
    n_attn = 3 * n_attn_w
    n_gdn = 4 * n_gdn_w
    assert w_in.shape[2] == n_attn + n_gdn + 2 * gdn_heads
    assert S % GDN_BLOCK == 0

    slopes = jnp.asarray(2.0 ** (-8.0 * np.arange(1, attn_heads + 1) / attn_heads), F32)
    x2 = x.reshape(T, D)
    for layer in range(depth):
        lambda_init = 0.8 - 0.6 * math.exp(-0.3 * layer)
        mod = _mod_call(c, w_mod[layer].astype(BF16), b_mod[layer])
        mod3 = mod.reshape(B, MOD_CHUNKS, D)

        w_main = w_in[layer, :, :n_attn + n_gdn].astype(BF16)
        w_ba_t = w_in[layer, :, n_attn + n_gdn:].T.astype(BF16)
        qkv_a, qkvz_g, ba_row = _inproj_call(x2, mod3, pre_mix_gain[layer], w_main, w_ba_t,
                                             seq=S, n_attn=n_attn, n_gdn=n_gdn, tm=_tile(S, 512))

        lam_params = jnp.stack([lambda_q1[layer], lambda_k1[layer], lambda_q2[layer], lambda_k2[layer]])
        oa = _attn_call(slopes, qkv_a, lam_params, attn_norm_gain[layer], batch=B, seq=S, heads=attn_heads,
                        lambda_init=lambda_init, blk=_tile(S, 256))
        og = _gdn_call(a_log[layer], dt_bias[layer], qkvz_g, conv_w[layer], ba_row, gdn_norm_gain[layer],
                       batch=B, seq=S, heads=gdn_heads)

        w_fold = _fold_call(w_query[layer], sub_keys[layer].reshape(PEER_HEADS * 2, N_KEYS, PEER_HALF))
        x1, h2, scores = _outproj_call(oa, og, x2, mod3, post_mix_gain[layer], pre_ffn_gain[layer],
                                       w_out[layer].astype(BF16), w_fold, seq=S, tm=_tile(S, 512))
        gates, ki, kj = _topk_call(scores, tt=_tile(T, 512))
        wtab = _wbuild_call(gates, ki, kj, tmw=_tile(T, 256))
        x2 = _expert_call(h2, expert_down[layer].astype(BF16), expert_up[layer].astype(BF16), wtab, x1, mod3,
                          post_ffn_gain[layer], seq=S, tm=_tile(S, 1024), ic=8)
    return x2.reshape(B, S, D)
```

```python
import functools
import math

import jax
import jax.numpy as jnp
import numpy as np
from jax import lax
from jax.experimental import pallas as pl
from jax.experimental.pallas import tpu as pltpu

F32 = jnp.float32
BF16 = jnp.bfloat16
I32 = jnp.int32

NORM_EPS = 1e-6
NEG_INF = -1e30

DIFF_HEAD_DIM = 128
DIFF_QK_DIM = 64
GDN_HEAD_DIM = 128
CONV_WIDTH = 4
GDN_CHUNK = 64
N_KEYS = 128
PEER_HEADS = 8
PEER_TOPK = 16
PEER_KEY_DIM = 256
PEER_HALF = 128
MOD_CHUNKS = 6

LANES = 128
SUBLANES = 8
VMEM_LIMIT_BYTES = 56 * 1024 * 1024

NT_DIMS = (((1,), (1,)), ((), ()))


def _dot(a, b):
    return jnp.dot(a, b, preferred_element_type=F32)


def _dot_nt(a, b):
    return lax.dot_general(a, b, NT_DIMS, preferred_element_type=F32)


def _split_bf16(a):
    hi = a.astype(BF16)
    lo = (a - hi.astype(F32)).astype(BF16)
    return hi, lo


def _dot_hi(a, b):
    ah, al = _split_bf16(a)
    bh, bl = _split_bf16(b)
    return _dot(ah, bh) + _dot(ah, bl) + _dot(al, bh)


def _rms(x):
    return x * lax.rsqrt(jnp.mean(x * x, axis=-1, keepdims=True) + NORM_EPS)


def _params(*sem):
    return pltpu.CompilerParams(dimension_semantics=sem, vmem_limit_bytes=VMEM_LIMIT_BYTES)


def _mod_kernel(c_ref, w_ref, b_ref, o_ref):
    c = c_ref[...]
    sc = c * jax.nn.sigmoid(c)
    o_ref[...] = _dot(sc.astype(BF16), w_ref[...]) + b_ref[...]


def _mod_call(c, w_mod_bf16, b_mod):
    B, D = c.shape
    n = w_mod_bf16.shape[1]
    return pl.pallas_call(
        _mod_kernel,
        out_shape=jax.ShapeDtypeStruct((B, n), F32),
        grid=(n // D,),
        in_specs=[
            pl.BlockSpec((B, D), lambda j: (0, 0)),
            pl.BlockSpec((D, D), lambda j: (0, j)),
            pl.BlockSpec((1, D), lambda j: (0, j)),
        ],
        out_specs=pl.BlockSpec((B, D), lambda j: (0, j)),
        compiler_params=_params("arbitrary"),
        name="mod",
    )(c, w_mod_bf16, b_mod.reshape(1, n))


def _inproj_kernel(x_ref, mod_ref, gain_ref, w_ref, wba_ref, oa_ref, og_ref, oba_ref, *, n_attn, n_gdn, n_step):
    m = mod_ref[0]
    h = _rms(x_ref[...]) * gain_ref[...]
    h = h * (1.0 + m[1:2]) + m[0:1]
    hb = h.astype(BF16)
    for n0 in range(0, n_attn, n_step):
        oa_ref[:, n0:n0 + n_step] = _dot(hb, w_ref[:, n0:n0 + n_step]).astype(BF16)
    for n0 in range(0, n_gdn, n_step):
        og_ref[:, n0:n0 + n_step] = _dot(hb, w_ref[:, n_attn + n0:n_attn + n0 + n_step]).astype(BF16)
    oba_ref[...] = _dot_nt(wba_ref[...], hb)


def _inproj_call(x2, mod3, gain, w_main, w_ba_t, *, seq, n_attn, n_gdn, tm):
    T, D = x2.shape
    tiles_per_b = seq // tm
    nba = w_ba_t.shape[0]
    kern = functools.partial(_inproj_kernel, n_attn=n_attn, n_gdn=n_gdn, n_step=512)
    return pl.pallas_call(
        kern,
        out_shape=(
            jax.ShapeDtypeStruct((T, n_attn), BF16),
            jax.ShapeDtypeStruct((T, n_gdn), BF16),
            jax.ShapeDtypeStruct((nba, T), F32),
        ),
        grid=(T // tm,),
        in_specs=[
            pl.BlockSpec((tm, D), lambda i: (i, 0)),
            pl.BlockSpec((1, MOD_CHUNKS, D), lambda i: (i // tiles_per_b, 0, 0)),
            pl.BlockSpec((1, D), lambda i: (0, 0)),
            pl.BlockSpec((D, n_attn + n_gdn), lambda i: (0, 0)),
            pl.BlockSpec((nba, D), lambda i: (0, 0)),
        ],
        out_specs=(
            pl.BlockSpec((tm, n_attn), lambda i: (i, 0)),
            pl.BlockSpec((tm, n_gdn), lambda i: (i, 0)),
            pl.BlockSpec((nba, tm), lambda i: (0, i)),
        ),
        compiler_params=_params("arbitrary"),
        name="inproj",
    )(x2, mod3, gain.reshape(1, D), w_main, w_ba_t)


ATTN_ROWS = 32


def _attn_kernel(slopes_ref, q_ref, k_ref, v_ref, lam_ref, gain_ref, o_ref,
                 s_scr, p_scr, m_scr, l_scr, c_scr, acc_scr, *, lambda_init, blk):
    h = pl.program_id(1)
    qi = pl.program_id(2)
    slope = slopes_ref[h]
    lp = lam_ref[...]
    lam = (jnp.exp(jnp.sum(lp[0:1] * lp[1:2], axis=-1, keepdims=True))
           - jnp.exp(jnp.sum(lp[2:3] * lp[3:4], axis=-1, keepdims=True)) + lambda_init)

    q = q_ref[...] * (DIFF_QK_DIM ** -0.5)
    lane = lax.broadcasted_iota(I32, q.shape, 1)
    zero = jnp.zeros_like(q)
    q0 = jnp.where(lane < DIFF_QK_DIM, q, zero)
    q1 = jnp.where(lane >= DIFF_QK_DIM, q, zero)

    m_scr[...] = jnp.full(m_scr.shape, NEG_INF, F32)
    l_scr[...] = jnp.zeros(l_scr.shape, F32)
    acc_scr[...] = jnp.zeros(acc_scr.shape, F32)

    def block(j, masked):
        start = pl.multiple_of(j * blk, blk)
        k = k_ref[pl.ds(start, blk), :]
        v = v_ref[pl.ds(start, blk), :]
        s_scr[0:blk, :] = _dot_nt(q0, k)
        s_scr[blk:2 * blk, :] = _dot_nt(q1, k)
        kpos = start + lax.broadcasted_iota(I32, (1, blk), 1)
        bias = slope * kpos.astype(F32)

        def score_tiles(r0, sub):
            s = s_scr[sub * blk + r0:sub * blk + r0 + ATTN_ROWS, :] + bias
            if masked:
                qpos = qi * blk + r0 + lax.broadcasted_iota(I32, (ATTN_ROWS, 1), 0)
                s = jnp.where(kpos <= qpos, s, NEG_INF)
            return [s[:, c0:c0 + LANES] for c0 in range(0, blk, LANES)]

        groups = [(r0, sub) for r0 in range(0, blk, ATTN_ROWS) for sub in range(2)]
        for r0, sub in groups:
            rs = slice(sub * blk + r0, sub * blk + r0 + ATTN_ROWS)
            tile_max = functools.reduce(jnp.maximum, score_tiles(r0, sub))
            m_old = m_scr[rs, :]
            m_new = jnp.maximum(m_old, jnp.max(tile_max, axis=-1, keepdims=True))
            c_scr[rs, :] = jnp.exp(m_old - m_new)
            m_scr[rs, :] = m_new
        for r0, sub in groups:
            rs = slice(sub * blk + r0, sub * blk + r0 + ATTN_ROWS)
            m = m_scr[rs, :]
            p = [jnp.exp(t - m) for t in score_tiles(r0, sub)]
            tile_sum = functools.reduce(jnp.add, p)
            l_scr[rs, :] = c_scr[rs, :] * l_scr[rs, :] + jnp.sum(tile_sum, axis=-1, keepdims=True)
            p_scr[rs, :] = jnp.concatenate(p, axis=1).astype(BF16)
        acc_scr[...] = c_scr[...] * acc_scr[...] + _dot(p_scr[...], v)

    def unmasked(j, carry):
        block(j, False)
        return carry

    lax.fori_loop(0, qi, unmasked, 0)
    block(qi, True)
    o = acc_scr[0:blk, :] / l_scr[0:blk, :] - lam * (acc_scr[blk:2 * blk, :] / l_scr[blk:2 * blk, :])
    o = _rms(o) * gain_ref[...] * (1.0 - lambda_init)
    o_ref[...] = o.astype(BF16)


def _attn_call(slopes, qkv, lam_params, gain, *, batch, seq, heads, lambda_init, blk):
    T = qkv.shape[0]
    nq = seq // blk
    assert blk % ATTN_ROWS == 0
    kern = functools.partial(_attn_kernel, lambda_init=lambda_init, blk=blk)
    return pl.pallas_call(
        kern,
        out_shape=jax.ShapeDtypeStruct((T, heads * DIFF_HEAD_DIM), BF16),
        grid=(batch, heads, nq),
        in_specs=[
            pl.BlockSpec(memory_space=pltpu.SMEM),
            pl.BlockSpec((blk, DIFF_HEAD_DIM), lambda b, h, i: (b * nq + i, h)),
            pl.BlockSpec((seq, DIFF_HEAD_DIM), lambda b, h, i: (b, heads + h)),
            pl.BlockSpec((seq, DIFF_HEAD_DIM), lambda b, h, i: (b, 2 * heads + h)),
            pl.BlockSpec((4, DIFF_QK_DIM), lambda b, h, i: (0, 0)),
            pl.BlockSpec((1, DIFF_HEAD_DIM), lambda b, h, i: (0, 0)),
        ],
        out_specs=pl.BlockSpec((blk, DIFF_HEAD_DIM), lambda b, h, i: (b * nq + i, h)),
        scratch_shapes=[
            pltpu.VMEM((2 * blk, blk), F32),
            pltpu.VMEM((2 * blk, blk), BF16),
            pltpu.VMEM((2 * blk, LANES), F32),
            pltpu.VMEM((2 * blk, LANES), F32),
            pltpu.VMEM((2 * blk, LANES), F32),
            pltpu.VMEM((2 * blk, DIFF_HEAD_DIM), F32),
        ],
        compiler_params=_params("arbitrary", "arbitrary", "arbitrary"),
        name="diff_attn",
    )(slopes, qkv, qkv, qkv, lam_params, gain.reshape(1, DIFF_HEAD_DIM))


GDN_BLOCK = 2 * GDN_CHUNK
INV_BLOCK = 16


def _each(f, *lists):
    return [f(*args) for args in zip(*lists)]


def _unit_lower_inverse(Ls, eye, in_blk):
    Ld = _each(lambda L: jnp.where(in_blk, L, 0.0), Ls)
    Lo = _each(lambda L, ld: L - ld, Ls, Ld)
    P = _each(lambda ld: eye - ld, Ld)
    X = _each(_dot_hi, Ld, Ld)
    for _ in range(2):
        P = _each(lambda p, x: p + _dot_hi(p, x), P, X)
        X = _each(_dot_hi, X, X)
    Td = _each(lambda p, x: p + _dot_hi(p, x), P, X)
    M = _each(_dot_hi, Td, Lo)
    M2 = _each(_dot_hi, M, M)
    R = _each(lambda m: eye - m, M)
    R = _each(lambda r, m2: r + _dot_hi(r, m2), R, M2)
    return _each(_dot_hi, R, Td)


def _gdn_kernel(alog_ref, dtb_ref, q_ref, k_ref, v_ref, z_ref, cq_ref, ck_ref, cv_ref, ba_ref, gain_ref,
                o_ref, qs, ks, vs, *, heads):
    seq = q_ref.shape[0]
    dk = GDN_HEAD_DIM
    nb = seq // GDN_BLOCK
    C = GDN_CHUNK

    row = lax.broadcasted_iota(I32, (seq, dk), 0)

    def conv_silu(x_ref, w_ref, lo):
        x = x_ref[:, lo:lo + dk].astype(F32)
        w = w_ref[:, lo:lo + dk]
        y = w[CONV_WIDTH - 1:CONV_WIDTH] * x
        for j in range(CONV_WIDTH - 1):
            s = CONV_WIDTH - 1 - j
            xs = jnp.where(row >= s, pltpu.roll(x, s, 0), 0.0)
            y = y + w[j:j + 1] * xs
        return y * jax.nn.sigmoid(y)

    def l2n(x):
        return x * lax.rsqrt(jnp.sum(x * x, axis=-1, keepdims=True) + NORM_EPS)

    for hd in range(heads):
        lo = hd * dk
        qs[:, lo:lo + dk] = l2n(conv_silu(q_ref, cq_ref, lo)) * (dk ** -0.5)
        ks[:, lo:lo + dk] = l2n(conv_silu(k_ref, ck_ref, lo))
        vs[:, lo:lo + dk] = conv_silu(v_ref, cv_ref, lo)

    ii = lax.broadcasted_iota(I32, (GDN_BLOCK, GDN_BLOCK), 0)
    jj = lax.broadcasted_iota(I32, (GDN_BLOCK, GDN_BLOCK), 1)
    same = (ii // C) == (jj // C)
    incl = same & (jj <= ii)
    strict = same & (jj < ii)
    upper = same & (ii <= jj)
    eye_b = ii == jj
    eye = eye_b.astype(F32)
    in_blk = (ii // INV_BLOCK) == (jj // INV_BLOCK)
    last = same & ((jj % C) == (C - 1))

    a_coef = [jnp.exp(jnp.full((1, GDN_BLOCK), alog_ref[hd], F32)) for hd in range(heads)]
    dt_b = [jnp.full((1, GDN_BLOCK), dtb_ref[hd], F32) for hd in range(heads)]
    gain = gain_ref[...]
    zeros_c = jnp.zeros((C, dk), F32)

    def gates(hd, ba):
        beta_row = jax.nn.sigmoid(ba[hd:hd + 1])
        g_row = -a_coef[hd] * jax.nn.softplus(ba[heads + hd:heads + hd + 1] + dt_b[hd])
        g_b = jnp.broadcast_to(g_row, (GDN_BLOCK, GDN_BLOCK))
        beta_b = jnp.broadcast_to(beta_row, (GDN_BLOCK, GDN_BLOCK))
        cs_col = jnp.sum(jnp.where(incl, g_b, 0.0), axis=-1, keepdims=True)
        g_col = jnp.sum(jnp.where(eye_b, g_b, 0.0), axis=-1, keepdims=True)
        beta_col = jnp.sum(jnp.where(eye_b, beta_b, 0.0), axis=-1, keepdims=True)
        cs_row = jnp.sum(jnp.where(upper, g_col, 0.0), axis=0, keepdims=True)
        gl_col = jnp.sum(jnp.where(last, cs_row, 0.0), axis=-1, keepdims=True)
        decay = jnp.where(incl, jnp.exp(jnp.where(incl, cs_col - cs_row, 0.0)), 0.0)
        return beta_col, cs_col, gl_col, decay

    def chunk_step(r0, prev_vn, state, u, wb, qg, qk, kdec, egl):
        sb = _each(lambda s: s.astype(BF16), state)
        vn = _each(lambda u_, w_, s_: u_[r0:r0 + C] - _dot(w_[r0:r0 + C], s_), u, wb, sb)
        vfull = _each(lambda p, v_: jnp.concatenate([v_, zeros_c] if p is None else [p, v_], axis=0).astype(BF16),
                      prev_vn, vn)
        o = _each(lambda g_, s_, k_, vf: _dot(g_[r0:r0 + C], s_) + _dot(k_[r0:r0 + C], vf), qg, sb, qk, vfull)
        state = _each(lambda s, e, kd, v_: s * e[r0:r0 + 1] + _dot(kd[r0:r0 + C].T.astype(BF16), v_.astype(BF16)),
                      state, egl, kdec, vn)
        return vn, o, state

    def body(n, states):
        st = pl.multiple_of(n * GDN_BLOCK, GDN_BLOCK)
        ba = ba_ref[:, pl.ds(st, GDN_BLOCK)]
        hs = list(range(heads))
        beta_col, cs_col, gl_col, decay = zip(*[gates(hd, ba) for hd in hs])
        q = [qs[pl.ds(st, GDN_BLOCK), hd * dk:(hd + 1) * dk] for hd in hs]
        k = [ks[pl.ds(st, GDN_BLOCK), hd * dk:(hd + 1) * dk] for hd in hs]
        v = [vs[pl.ds(st, GDN_BLOCK), hd * dk:(hd + 1) * dk] for hd in hs]
        kb = _each(lambda k_, b: k_ * b, k, beta_col)
        kbf = _each(lambda k_: k_.astype(BF16), k)
        L = _each(lambda kb_, kf, d: jnp.where(strict, _dot_nt(kb_.astype(BF16), kf) * d, 0.0), kb, kbf, decay)
        T = _unit_lower_inverse(L, eye, in_blk)
        rhs = _each(lambda v_, b, kb_, c: jnp.concatenate([v_ * b, kb_ * jnp.exp(c)], axis=1),
                    v, beta_col, kb, cs_col)
        sol = _each(_dot_hi, T, rhs)
        u = _each(lambda s: s[:, :dk], sol)
        wb = _each(lambda s: s[:, dk:].astype(BF16), sol)
        qk = _each(lambda q_, kf, d: (_dot_nt(q_.astype(BF16), kf) * d).astype(BF16), q, kbf, decay)
        qg = _each(lambda q_, c: (q_ * jnp.exp(c)).astype(BF16), q, cs_col)
        kdec = _each(lambda k_, gl, c: k_ * jnp.exp(gl - c), k, gl_col, cs_col)
        egl = _each(jnp.exp, gl_col)

        vn0, o0, states = chunk_step(0, [None] * heads, list(states), u, wb, qg, qk, kdec, egl)
        _, o1, states = chunk_step(C, vn0, states, u, wb, qg, qk, kdec, egl)

        for hd in hs:
            o = jnp.concatenate([o0[hd], o1[hd]], axis=0)
            z = z_ref[pl.ds(st, GDN_BLOCK), hd * dk:(hd + 1) * dk].astype(F32)
            o = _rms(o) * gain * (z * jax.nn.sigmoid(z))
            o_ref[pl.ds(st, GDN_BLOCK), hd * dk:(hd + 1) * dk] = o.astype(BF16)
        return tuple(states)

    lax.fori_loop(0, nb, body, tuple(jnp.zeros((dk, dk), F32) for _ in range(heads)))


def _gdn_call(a_log, dt_bias, gq, conv_w, ba_row, gain, *, batch, seq, heads):
    T = gq.shape[0]
    dk = GDN_HEAD_DIM
    wd = heads * dk
    kern = functools.partial(_gdn_kernel, heads=heads)
    seq_spec = lambda blk: pl.BlockSpec((seq, wd), lambda b: (b, blk))
    conv_spec = lambda blk: pl.BlockSpec((CONV_WIDTH, wd), lambda b: (0, blk))
    return pl.pallas_call(
        kern,
        out_shape=jax.ShapeDtypeStruct((T, wd), BF16),
        grid=(batch,),
        in_specs=[
            pl.BlockSpec(memory_space=pltpu.SMEM),
            pl.BlockSpec(memory_space=pltpu.SMEM),
            seq_spec(0), seq_spec(1), seq_spec(2), seq_spec(3),
            conv_spec(0), conv_spec(1), conv_spec(2),
            pl.BlockSpec((2 * heads, seq), lambda b: (0, b)),
            pl.BlockSpec((1, dk), lambda b: (0, 0)),
        ],
        out_specs=pl.BlockSpec((seq, wd), lambda b: (b, 0)),
        scratch_shapes=[pltpu.VMEM((seq, wd), F32)] * 3,
        compiler_params=_params("arbitrary"),
        name="gdn",
    )(a_log, dt_bias, gq, gq, gq, gq, conv_w, conv_w, conv_w, ba_row, gain.reshape(1, dk))


def _fold_kernel(wq_ref, sk_ref, o_ref):
    ah, al = _split_bf16(wq_ref[...])
    bh, bl = _split_bf16(sk_ref[0])
    o_ref[...] = (_dot_nt(ah, bh) + _dot_nt(ah, bl) + _dot_nt(al, bh)).astype(BF16)


def _fold_call(w_query, sub_keys):
    D = w_query.shape[0]
    nhc = sub_keys.shape[0]
    return pl.pallas_call(
        _fold_kernel,
        out_shape=jax.ShapeDtypeStruct((D, nhc * N_KEYS), BF16),
        grid=(nhc,),
        in_specs=[
            pl.BlockSpec((D, PEER_HALF), lambda j: (0, j)),
            pl.BlockSpec((1, N_KEYS, PEER_HALF), lambda j: (j, 0, 0)),
        ],
        out_specs=pl.BlockSpec((D, N_KEYS), lambda j: (0, j)),
        compiler_params=_params("arbitrary"),
        name="peer_fold",
    )(w_query, sub_keys)


def _outproj_kernel(oa_ref, og_ref, x_ref, mod_ref, pg_ref, fg_ref, wo_ref, wf_ref,
                    x1_ref, h2_ref, sc_ref, *, n_step):
    m = mod_ref[0]
    y = _dot(jnp.concatenate([oa_ref[...], og_ref[...]], axis=1), wo_ref[...])
    x1 = x_ref[...] + m[2:3] * (_rms(y) * pg_ref[...])
    x1_ref[...] = x1
    h2 = (_rms(x1) * fg_ref[...]) * (1.0 + m[4:5]) + m[3:4]
    h2b = h2.astype(BF16)
    h2_ref[...] = h2b
    for n0 in range(0, wf_ref.shape[1], n_step):
        sc_ref[:, n0:n0 + n_step] = _dot(h2b, wf_ref[:, n0:n0 + n_step])


def _outproj_call(oa, og, x2, mod3, post_gain, ffn_gain, w_out, w_fold, *, seq, tm):
    T, D = x2.shape
    tiles_per_b = seq // tm
    ns = w_fold.shape[1]
    a = oa.shape[1]
    g = og.shape[1]
    return pl.pallas_call(
        functools.partial(_outproj_kernel, n_step=512),
        out_shape=(
            jax.ShapeDtypeStruct((T, D), F32),
            jax.ShapeDtypeStruct((T, D), BF16),
            jax.ShapeDtypeStruct((T, ns), F32),
        ),
        grid=(T // tm,),
        in_specs=[
            pl.BlockSpec((tm, a), lambda i: (i, 0)),
            pl.BlockSpec((tm, g), lambda i: (i, 0)),
            pl.BlockSpec((tm, D), lambda i: (i, 0)),
            pl.BlockSpec((1, MOD_CHUNKS, D), lambda i: (i // tiles_per_b, 0, 0)),
            pl.BlockSpec((1, D), lambda i: (0, 0)),
            pl.BlockSpec((1, D), lambda i: (0, 0)),
            pl.BlockSpec((a + g, D), lambda i: (0, 0)),
            pl.BlockSpec((D, ns), lambda i: (0, 0)),
        ],
        out_specs=(
            pl.BlockSpec((tm, D), lambda i: (i, 0)),
            pl.BlockSpec((tm, D), lambda i: (i, 0)),
            pl.BlockSpec((tm, ns), lambda i: (i, 0)),
        ),
        compiler_params=_params("arbitrary"),
        name="outproj",
    )(oa, og, x2, mod3, post_gain.reshape(1, D), ffn_gain.reshape(1, D), w_out, w_fold)


N_CAND_ROWS = 80


def _topk_kernel(sc_ref, g_ref, i_ref, j_ref, gt, it, jt, *, cols):
    K = PEER_TOPK
    neg = jnp.float32(-jnp.inf)
    n_iota = lax.broadcasted_iota(I32, (N_KEYS, LANES), 0).astype(F32)
    r16 = lax.broadcasted_iota(I32, (K, LANES), 0)

    r = lax.broadcasted_iota(I32, (N_CAND_ROWS, LANES), 0)
    grp = r // SUBLANES
    sub = r % SUBLANES
    cand_a = jnp.where(grp < 8, grp, jnp.where(grp == 8, 0, 8 + sub))
    cand_b = jnp.where(grp < 8, sub, jnp.where(grp == 8, 8 + sub, 0))
    cand_ok = (cand_a + 1) * (cand_b + 1) <= K
    cand_pos = (cand_a * K + cand_b).astype(F32)

    def top16(s):
        def step(k, carry):
            cur, vals, idxs = carry
            m = jnp.max(cur, axis=0, keepdims=True)
            idx = jnp.min(jnp.where(cur == m, n_iota, float(N_KEYS)), axis=0, keepdims=True)
            vals = jnp.where(r16 == k, m, vals)
            idxs = jnp.where(r16 == k, idx, idxs)
            cur = jnp.where(n_iota == idx, neg, cur)
            return cur, vals, idxs
        _, vals, idxs = lax.fori_loop(0, K, step, (s, jnp.zeros((K, LANES), F32), jnp.zeros((K, LANES), F32)),
                                      unroll=True)
        return vals, idxs

    def keys_by_tokens(hc, c0):
        return sc_ref[c0:c0 + LANES, hc * N_KEYS:(hc + 1) * N_KEYS].T

    def per_head(hh, c0):
        v0, i0 = top16(keys_by_tokens(2 * hh, c0))
        v1, i1 = top16(keys_by_tokens(2 * hh + 1, c0))
        groups = [v0[a:a + 1] + v1[0:SUBLANES] for a in range(8)]
        groups.append(v0[0:1] + v1[SUBLANES:K])
        groups.append(v0[SUBLANES:K] + v1[0:1])
        cand = jnp.where(cand_ok, jnp.concatenate(groups, axis=0), neg)

        def step(k, carry):
            cur, vals, poss = carry
            m = jnp.max(cur, axis=0, keepdims=True)
            p = jnp.min(jnp.where(cur == m, cand_pos, float(K * K)), axis=0, keepdims=True)
            vals = jnp.where(r16 == k, m, vals)
            poss = jnp.where(r16 == k, p, poss)
            cur = jnp.where(cand_pos == p, neg, cur)
            return cur, vals, poss
        _, best, pos = lax.fori_loop(0, K, step, (cand, jnp.zeros((K, LANES), F32), jnp.zeros((K, LANES), F32)),
                                     unroll=True)
        pos = pos.astype(I32)
        pa = pos // K
        pb = pos % K
        ki = jnp.zeros((K, LANES), F32)
        kj = jnp.zeros((K, LANES), F32)
        for a in range(K):
            ki = jnp.where(pa == a, i0[a:a + 1], ki)
            kj = jnp.where(pb == a, i1[a:a + 1], kj)
        e = jnp.exp(best - best[0:1])
        gates = e / jnp.sum(e, axis=0, keepdims=True)
        return gates, ki, kj

    for c in range(cols):
        c0 = c * LANES
        for hh in range(PEER_HEADS):
            gates, ki, kj = per_head(hh, c0)
            gt[hh * K:(hh + 1) * K, c0:c0 + LANES] = gates
            it[hh * K:(hh + 1) * K, c0:c0 + LANES] = ki
            jt[hh * K:(hh + 1) * K, c0:c0 + LANES] = kj
    g_ref[...] = gt[...].T
    i_ref[...] = it[...].T.astype(I32)
    j_ref[...] = jt[...].T.astype(I32)


def _topk_call(scores, *, tt):
    T, ns = scores.shape
    hk = PEER_HEADS * PEER_TOPK
    kern = functools.partial(_topk_kernel, cols=tt // LANES)
    return pl.pallas_call(
        kern,
        out_shape=(
            jax.ShapeDtypeStruct((T, hk), F32),
            jax.ShapeDtypeStruct((T, hk), I32),
            jax.ShapeDtypeStruct((T, hk), I32),
        ),
        grid=(T // tt,),
        in_specs=[pl.BlockSpec((tt, ns), lambda i: (i, 0))],
        out_specs=(
            pl.BlockSpec((tt, hk), lambda i: (i, 0)),
            pl.BlockSpec((tt, hk), lambda i: (i, 0)),
            pl.BlockSpec((tt, hk), lambda i: (i, 0)),
        ),
        scratch_shapes=[pltpu.VMEM((hk, tt), F32)] * 3,
        compiler_params=_params("arbitrary"),
        name="peer_topk",
    )(scores)


GATE_TILE = 256
GATE_PITCH = GATE_TILE // 2 + SUBLANES


def _wbuild_kernel(g_ref, i_ref, j_ref, w_ref):
    sub = lax.broadcasted_iota(I32, (N_KEYS, LANES), 0).astype(F32).astype(BF16)

    def token_tile(t):
        g = g_ref[pl.ds(t, 1), :].astype(BF16)
        ki = i_ref[pl.ds(t, 1), :].astype(F32).astype(BF16)
        kj = j_ref[pl.ds(t, 1), :].astype(F32).astype(BF16)
        pi = jnp.where(sub == ki, g, jnp.zeros_like(g))
        pj = jnp.where(sub == kj, jnp.ones_like(g), jnp.zeros_like(g))
        return _dot_nt(pi, pj)

    def body(p, carry):
        t = 2 * p
        lo = lax.bitcast_convert_type(token_tile(t).astype(BF16).astype(F32), jnp.uint32) >> 16
        hi = lax.bitcast_convert_type(token_tile(t + 1).astype(BF16).astype(F32), jnp.uint32) & jnp.uint32(0xFFFF0000)
        words = lo | hi
        w_ref[0, pl.ds(p, N_KEYS, stride=GATE_PITCH), :] = words
        return carry

    lax.fori_loop(0, GATE_TILE // 2, body, 0, unroll=16)


def _wbuild_call(gates, ki, kj):
    T, hk = gates.shape
    assert T % GATE_TILE == 0
    return pl.pallas_call(
        _wbuild_kernel,
        out_shape=jax.ShapeDtypeStruct((T // GATE_TILE, N_KEYS * GATE_PITCH, N_KEYS), jnp.uint32),
        grid=(T // GATE_TILE,),
        in_specs=[pl.BlockSpec((GATE_TILE, hk), lambda i: (i, 0))] * 3,
        out_specs=pl.BlockSpec((1, N_KEYS * GATE_PITCH, N_KEYS), lambda i: (i, 0, 0)),
        compiler_params=_params("arbitrary"),
        name="peer_gate_table",
    )(gates, ki, kj)


def _expert_kernel(h_ref, d_ref, u_ref, w_ref, x1_ref, mod_ref, gain_ref, o_ref, acc, *, ic):
    e = pl.program_id(1)

    @pl.when(e == 0)
    def _():
        acc[...] = jnp.zeros_like(acc)

    s = _dot_nt(h_ref[...], d_ref[...])
    act = 0.5 * s * (1.0 + lax.erf(s * (2.0 ** -0.5)))
    nt = w_ref.shape[0]
    w = jnp.concatenate(
        [jnp.concatenate([pltpu.bitcast(w_ref[tt, i * GATE_PITCH:i * GATE_PITCH + GATE_TILE // 2, :], BF16)
                          for tt in range(nt)], axis=0) for i in range(ic)], axis=1).astype(F32)
    acc[...] += _dot((act * w).astype(BF16), u_ref[...])

    @pl.when(e == pl.num_programs(1) - 1)
    def _():
        m = mod_ref[0]
        o_ref[...] = x1_ref[...] + m[5:6] * (_rms(acc[...]) * gain_ref[...])


def _expert_call(h2, down, up, wtab, x1, mod3, gain, *, seq, tm, ic):
    T, D = h2.shape
    tiles_per_b = seq // tm
    ec = ic * N_KEYS
    kern = functools.partial(_expert_kernel, ic=ic)
    return pl.pallas_call(
        kern,
        out_shape=jax.ShapeDtypeStruct((T, D), F32),
        grid=(T // tm, N_KEYS // ic),
        in_specs=[
            pl.BlockSpec((tm, D), lambda i, e: (i, 0)),
            pl.BlockSpec((ec, D), lambda i, e: (e, 0)),
            pl.BlockSpec((ec, D), lambda i, e: (e, 0)),
            pl.BlockSpec((tm // GATE_TILE, ic * GATE_PITCH, N_KEYS), lambda i, e: (i, e, 0)),
            pl.BlockSpec((tm, D), lambda i, e: (i, 0)),
            pl.BlockSpec((1, MOD_CHUNKS, D), lambda i, e: (i // tiles_per_b, 0, 0)),
            pl.BlockSpec((1, D), lambda i, e: (0, 0)),
        ],
        out_specs=pl.BlockSpec((tm, D), lambda i, e: (i, 0)),
        scratch_shapes=[pltpu.VMEM((tm, D), F32)],
        compiler_params=_params("arbitrary", "arbitrary"),
        name="peer_experts",
    )(h2, down, up, wtab, x1, mod3, gain.reshape(1, D))


def _tile(n, pref):
    t = min(n, pref)
    assert n % t == 0, (n, pref)
    return t


def kernel(x, c, w_mod, b_mod, pre_mix_gain, post_mix_gain, pre_ffn_gain, post_ffn_gain, w_in, conv_w, a_log, dt_bias, lambda_q1, lambda_k1, lambda_q2, lambda_k2, attn_norm_gain, gdn_norm_gain, w_out, w_query, sub_keys, expert_down, expert_up):
    B, S, D = x.shape
    T = B * S
    depth = w_mod.shape[0]
    n_attn_w = D // 2
    n_gdn_w = D - n_attn_w
    attn_heads = n_attn_w // DIFF_HEAD_DIM
    gdn_heads = n_gdn_w // GDN_HEAD_DIM
    n_attn = 3 * n_attn_w
    n_gdn = 4 * n_gdn_w
    assert w_in.shape[2] == n_attn + n_gdn + 2 * gdn_heads
    assert S % GDN_BLOCK == 0

    slopes = jnp.asarray(2.0 ** (-8.0 * np.arange(1, attn_heads + 1) / attn_heads), F32)
    x2 = x.reshape(T, D)
    for layer in range(depth):
        lambda_init = 0.8 - 0.6 * math.exp(-0.3 * layer)
        mod = _mod_call(c, w_mod[layer].astype(BF16), b_mod[layer])
        mod3 = mod.reshape(B, MOD_CHUNKS, D)

        w_main = w_in[layer, :, :n_attn + n_gdn].astype(BF16)
        w_ba_t = w_in[layer, :, n_attn + n_gdn:].T.astype(BF16)
        qkv_a, qkvz_g, ba_row = _inproj_call(x2, mod3, pre_mix_gain[layer], w_main, w_ba_t,
                                             seq=S, n_attn=n_attn, n_gdn=n_gdn, tm=_tile(S, 512))

        lam_params = jnp.stack([lambda_q1[layer], lambda_k1[layer], lambda_q2[layer], lambda_k2[layer]])
        oa = _attn_call(slopes, qkv_a, lam_params, attn_norm_gain[layer], batch=B, seq=S, heads=attn_heads,
                        lambda_init=lambda_init, blk=_tile(S, 256))
        og = _gdn_call(a_log[layer], dt_bias[layer], qkvz_g, conv_w[layer], ba_row, gdn_norm_gain[layer],
                       batch=B, seq=S, heads=gdn_heads)

        w_fold = _fold_call(w_query[layer], sub_keys[layer].reshape(PEER_HEADS * 2, N_KEYS, PEER_HALF))
        x1, h2, scores = _outproj_call(oa, og, x2, mod3, post_mix_gain[layer], pre_ffn_gain[layer],
                                       w_out[layer].astype(BF16), w_fold, seq=S, tm=_tile(S, 512))
        gates, ki, kj = _topk_call(scores, tt=_tile(T, 512))
        wtab = _wbuild_call(gates, ki, kj)
        x2 = _expert_call(h2, expert_down[layer].astype(BF16), expert_up[layer].astype(BF16), wtab, x1, mod3,
                          post_ffn_gain[layer], seq=S, tm=_tile(S, 1024), ic=8)
    return x2.reshape(B, S, D)
```
